```python
import math
import jax, jax.numpy as jnp
from jax import lax
import numpy as np

D_MODEL = 1024
BATCH = 2
SEQ = 8192
DEPTH = 2

N_BRANCH = 4
BRANCH_W = D_MODEL // 4
S5_GROUP = 16
S5_GROUPS = BRANCH_W // S5_GROUP
S5_STATE = 64
HG_HEADS = 4
HG_DK = BRANCH_W // HG_HEADS
HG_DV = BRANCH_W // HG_HEADS
GLA_HEADS = 4
GLA_DV = BRANCH_W // GLA_HEADS
GLA_DK = GLA_DV // 2
GLA_RANK = 16
GLA_GATE_NORM = 16.0
XA_HEADS = 4
XA_HD = BRANCH_W // XA_HEADS
MEM_LEN = 256
D_FF = 128 * ((8 * D_MODEL // 3 + 127) // 128)
CONV_W = 3
CHUNK = 16
EPS = 1e-6
GATE_FLOOR = 1e-30
PROJ_WIDTHS = (
    BRANCH_W,
    HG_HEADS * HG_DK, HG_HEADS * HG_DK, HG_HEADS * HG_DV, HG_HEADS * HG_DV,
    GLA_HEADS * GLA_DK, GLA_HEADS * GLA_DK, GLA_HEADS * GLA_DV, GLA_RANK, GLA_HEADS * GLA_DV,
    XA_HEADS * XA_HD,
    N_BRANCH * D_MODEL,
)
PROJ_DIM = sum(PROJ_WIDTHS)

kernel_name = 'hybrid_gated_s5_hgrn2_gla_xattn'


def _layer(t, l):
    return t[l].astype(jnp.float32)


def rmsnorm(x, g):
    x = x.astype(jnp.float32)
    return x * lax.rsqrt(jnp.mean(x * x, axis=-1, keepdims=True) + EPS) * g


def head_rmsnorm(o, g):
    bsz, seq = o.shape[0], o.shape[1]
    o = o * lax.rsqrt(jnp.mean(o * o, axis=-1, keepdims=True) + EPS)
    return o.reshape(bsz, seq, -1) * g


def _complex_affine_combine(e1, e2):
    a1r, a1i, b1r, b1i = e1
    a2r, a2i, b2r, b2i = e2
    return (a2r * a1r - a2i * a1i,
            a2r * a1i + a2i * a1r,
            a2r * b1r - a2i * b1i + b2r,
            a2r * b1i + a2i * b1r + b2i)


def s5_branch(u, lam_re, lam_im, log_dt, b_re, b_im, c_re, c_im, d_skip, w_glu, b_glu):
    bsz, seq, _ = u.shape
    u = u.reshape(bsz, seq, S5_GROUPS, S5_GROUP)
    dt = jnp.exp(log_dt)[:, None]
    mag = jnp.exp(lam_re * dt)
    ang = lam_im * dt
    a_re = mag * jnp.cos(ang)
    a_im = mag * jnp.sin(ang)
    den = lam_re * lam_re + lam_im * lam_im
    z_re = ((a_re - 1.0) * lam_re + a_im * lam_im) / den
    z_im = (a_im * lam_re - (a_re - 1.0) * lam_im) / den
    bb_re = z_re[..., None] * b_re - z_im[..., None] * b_im
    bb_im = z_re[..., None] * b_im + z_im[..., None] * b_re
    bu_re = jnp.einsum('blgc,gpc->blgp', u, bb_re)
    bu_im = jnp.einsum('blgc,gpc->blgp', u, bb_im)
    a_re_b = jnp.broadcast_to(a_re, bu_re.shape)
    a_im_b = jnp.broadcast_to(a_im, bu_re.shape)
    _, _, s_re, s_im = lax.associative_scan(
        _complex_affine_combine, (a_re_b, a_im_b, bu_re, bu_im), axis=1)
    y = (jnp.einsum('blgp,gcp->blgc', s_re, c_re)
         - jnp.einsum('blgp,gcp->blgc', s_im, c_im)
         + d_skip.reshape(S5_GROUPS, S5_GROUP) * u)
    y = jax.nn.gelu(y.reshape(bsz, seq, BRANCH_W))
    return y * jax.nn.sigmoid(y @ w_glu + b_glu)


def chunked_gated_linear_attention(q, k, v, log_f):
    bsz, seq, heads, dk = q.shape
    dv = v.shape[-1]
    n = seq // CHUNK
    blk = lambda t: t.reshape(bsz, n, CHUNK, heads, t.shape[-1])
    q, k, v, log_f = blk(q), blk(k), blk(v), blk(log_f)
    b = jnp.cumsum(log_f, axis=2)
    causal = jnp.tril(jnp.ones((CHUNK, CHUNK), dtype=bool))[None, None, :, :, None, None]
    diff = b[:, :, :, None] - b[:, :, None, :]
    decay = jnp.where(causal, jnp.exp(jnp.where(causal, diff, 0.0)), 0.0)
    scores = jnp.einsum('bnihk,bnjhk,bnijhk->bnhij', q, k, decay)
    o_intra = jnp.einsum('bnhij,bnjhd->bnihd', scores, v)
    b_last = b[:, :, -1]
    k_dec = k * jnp.exp(b_last[:, :, None] - b)
    d_state = jnp.einsum('bnchk,bnchd->bnhkd', k_dec, v)

    def step(state, inp):
        g_n, ds_n = inp
        return jnp.exp(g_n)[..., None] * state + ds_n, state

    s0 = jnp.zeros((bsz, heads, dk, dv), jnp.float32)
    _, s_prev = lax.scan(step, s0, (jnp.moveaxis(b_last, 1, 0), jnp.moveaxis(d_state, 1, 0)))
    s_prev = jnp.moveaxis(s_prev, 0, 1)
    o_inter = jnp.einsum('bnchk,bnhkd->bnchd', q * jnp.exp(b), s_prev)
    return (o_intra + o_inter).reshape(bsz, seq, heads, dv)


def hgrn2_branch(q, f, i, o_gate, lb, norm_g):
    bsz, seq, _ = q.shape
    g = lb + (1.0 - lb) * jax.nn.sigmoid(f)
    log_g = jnp.log(jnp.maximum(g, GATE_FLOOR))
    k = (1.0 - lb) * jax.nn.sigmoid(-f)
    hs = lambda t, d: t.reshape(bsz, seq, HG_HEADS, d)
    o = chunked_gated_linear_attention(hs(q, HG_DK), hs(k, HG_DK), hs(i, HG_DV), hs(log_g, HG_DK))
    return head_rmsnorm(o, norm_g) * jax.nn.sigmoid(o_gate)


def gla_branch(q, k, v, r, o_gate, w_g2, b_g, norm_g):
    bsz, seq, _ = q.shape
    log_f = jax.nn.log_sigmoid(r @ w_g2 + b_g) / GLA_GATE_NORM
    hs = lambda t, d: t.reshape(bsz, seq, GLA_HEADS, d)
    o = chunked_gated_linear_attention(hs(q * GLA_DK ** -0.5, GLA_DK), hs(k, GLA_DK),
                                       hs(v, GLA_DV), hs(log_f, GLA_DK))
    return head_rmsnorm(o, norm_g) * jax.nn.silu(o_gate)


def memory_cross_attention(q, mem_n, w_kv):
    bsz, seq, _ = q.shape
    k, v = jnp.split(mem_n @ w_kv, 2, axis=-1)
    m = k.shape[1]
    q = q.reshape(bsz, seq, XA_HEADS, XA_HD)
    k = k.reshape(bsz, m, XA_HEADS, XA_HD)
    v = v.reshape(bsz, m, XA_HEADS, XA_HD)
    s = jnp.einsum('blhd,bmhd->bhlm', q, k) * XA_HD ** -0.5
    a = jax.nn.softmax(s, axis=-1)
    return jnp.einsum('bhlm,bmhd->blhd', a, v).reshape(bsz, seq, BRANCH_W)


def conv_ffn(h, w_in, conv_w, conv_b, w_out):
    a, u = jnp.split(h @ w_in, 2, axis=-1)
    a = lax.conv_general_dilated(a, conv_w[:, None, :], (1,), [(CONV_W - 1, 0)],
                                 dimension_numbers=('NWC', 'WIO', 'NWC'),
                                 feature_group_count=D_FF) + conv_b
    return (jax.nn.silu(a) * u) @ w_out


def setup_inputs(seed: int = 0) -> dict:
    key = jax.random.key(seed)
    ks = iter(jax.random.split(key, 32))

    def nrm(shape, scale):
        return jax.random.normal(next(ks), shape, jnp.float32) * scale

    def gain(shape):
        return 1.0 + nrm(shape, 0.02)

    L, G, P = DEPTH, S5_GROUPS, S5_STATE
    n = jnp.arange(P, dtype=jnp.float32)
    return {
        'x': nrm((BATCH, SEQ, D_MODEL), 1.0),
        'mem': nrm((BATCH, MEM_LEN, D_MODEL), 1.0),
        'pre_mix_g': gain((L, D_MODEL)),
        'w_in': nrm((L, D_MODEL, PROJ_DIM), D_MODEL ** -0.5),
        's5_lam_re': -0.5 * jnp.exp(nrm((L, G, P), 0.05)),
        's5_lam_im': math.pi * n + nrm((L, G, P), 0.05),
        's5_log_dt': jax.random.uniform(next(ks), (L, G), jnp.float32, math.log(1e-3), math.log(1e-1)),
        's5_b_re': nrm((L, G, P, S5_GROUP), (2 * S5_GROUP) ** -0.5),
        's5_b_im': nrm((L, G, P, S5_GROUP), (2 * S5_GROUP) ** -0.5),
        's5_c_re': nrm((L, G, S5_GROUP, P), P ** -0.5),
        's5_c_im': nrm((L, G, S5_GROUP, P), P ** -0.5),
        's5_d': nrm((L, BRANCH_W), 1.0),
        's5_w_glu': nrm((L, BRANCH_W, BRANCH_W), BRANCH_W ** -0.5),
        's5_b_glu': nrm((L, BRANCH_W), 0.01),
        'hg_lb_logits': nrm((L, HG_HEADS * HG_DK), 1.0),
        'hg_norm_g': gain((L, HG_HEADS * HG_DV)),
        'gla_w_g2': nrm((L, GLA_RANK, GLA_HEADS * GLA_DK), GLA_RANK ** -0.5),
        'gla_b_g': nrm((L, GLA_HEADS * GLA_DK), 0.1),
        'gla_norm_g': gain((L, GLA_HEADS * GLA_DV)),
        'mem_norm_g': gain((L, D_MODEL)),
        'w_mem_kv': nrm((L, D_MODEL, 2 * XA_HEADS * XA_HD), D_MODEL ** -0.5),
        'w_branch': nrm((L, N_BRANCH, BRANCH_W, D_MODEL), BRANCH_W ** -0.5),
        'w_out': nrm((L, D_MODEL, D_MODEL), D_MODEL ** -0.5),
        'post_mix_g': gain((L, D_MODEL)),
        'pre_ffn_g': gain((L, D_MODEL)),
        'w_ffn_in': nrm((L, D_MODEL, 2 * D_FF), D_MODEL ** -0.5),
        'ffn_conv_w': nrm((L, CONV_W, D_FF), CONV_W ** -0.5),
        'ffn_conv_b': nrm((L, D_FF), 0.01),
        'w_ffn_out': nrm((L, D_FF, D_MODEL), D_FF ** -0.5),
        'post_ffn_g': gain((L, D_MODEL)),
    }


def reference(x, mem, pre_mix_g, w_in, s5_lam_re, s5_lam_im, s5_log_dt, s5_b_re, s5_b_im,
              s5_c_re, s5_c_im, s5_d, s5_w_glu, s5_b_glu, hg_lb_logits, hg_norm_g,
              gla_w_g2, gla_b_g, gla_norm_g, mem_norm_g, w_mem_kv, w_branch, w_out,
              post_mix_g, pre_ffn_g, w_ffn_in, ffn_conv_w, ffn_conv_b, w_ffn_out, post_ffn_g):
    out_dtype = x.dtype
    x = x.astype(jnp.float32)
    mem = mem.astype(jnp.float32)
    bsz, seq, _ = x.shape
    split_at = np.cumsum(PROJ_WIDTHS)[:-1].tolist()
    lb_p = jax.nn.softmax(hg_lb_logits.astype(jnp.float32), axis=0)
    lower_bounds = jnp.cumsum(lb_p, axis=0) - lb_p[0]
    for l in range(DEPTH):
        h = rmsnorm(x, _layer(pre_mix_g, l))
        (s5_u, hg_q, hg_f, hg_i, hg_o, gla_q, gla_k, gla_v, gla_r, gla_o, xa_q,
         gate_logits) = jnp.split(h @ _layer(w_in, l), split_at, axis=-1)
        y_s5 = s5_branch(s5_u, _layer(s5_lam_re, l), _layer(s5_lam_im, l), _layer(s5_log_dt, l),
                         _layer(s5_b_re, l), _layer(s5_b_im, l), _layer(s5_c_re, l),
                         _layer(s5_c_im, l), _layer(s5_d, l), _layer(s5_w_glu, l),
                         _layer(s5_b_glu, l))
        y_hg = hgrn2_branch(hg_q, hg_f, hg_i, hg_o, lower_bounds[l], _layer(hg_norm_g, l))
        y_gla = gla_branch(gla_q, gla_k, gla_v, gla_r, gla_o, _layer(gla_w_g2, l),
                           _layer(gla_b_g, l), _layer(gla_norm_g, l))
        y_xa = memory_cross_attention(xa_q, rmsnorm(mem, _layer(mem_norm_g, l)), _layer(w_mem_kv, l))
        branches = jnp.stack([y_s5, y_hg, y_gla, y_xa], axis=2)
        branch_proj = jnp.einsum('blki,kid->blkd', branches, _layer(w_branch, l))
        gates = jax.nn.sigmoid(gate_logits.reshape(bsz, seq, N_BRANCH, D_MODEL))
        mixed = jnp.sum(gates * branch_proj, axis=2) @ _layer(w_out, l)
        x = x + rmsnorm(mixed, _layer(post_mix_g, l))
        h = rmsnorm(x, _layer(pre_ffn_g, l))
        f = conv_ffn(h, _layer(w_ffn_in, l), _layer(ffn_conv_w, l), _layer(ffn_conv_b, l),
                     _layer(w_ffn_out, l))
        x = x + rmsnorm(f, _layer(post_ffn_g, l))
    return x.astype(out_dtype)
```

```python
import functools
import math

import jax
import jax.numpy as jnp
from jax import lax
from jax.experimental import pallas as pl
from jax.experimental.pallas import tpu as pltpu

D_MODEL = 1024
N_BRANCH = 4
BRANCH_W = 256
S5_GROUPS = 16
S5_GROUP = 16
S5_STATE = 64
S5_W = S5_GROUPS * S5_STATE
HEADS = 4
HG_DK = 64
GLA_DK = 32
GLA_RANK = 16
GLA_GATE_NORM = 16.0
XA_HD = 64
HEAD_DV = 64
D_FF = 2816
EPS = 1e-6
GATE_FLOOR = 1e-30

LANES = 128
SUBLANES = 8
ROW_TILE = 256
CHUNK = 64
SCAN_PAD = ROW_TILE // 2
N_POW = int(math.log2(ROW_TILE))
S5_BLOCKS = S5_W // LANES

C_S5 = 0
C_HGQ, C_HGF, C_HGI, C_HGO = 256, 512, 768, 1024
C_GQ, C_GK, C_GV, C_GR, C_GO = 1280, 1408, 1536, 1792, 1920
C_XQ = 2176
C_GATE = 2432
PROJ_PAD = C_GATE + N_BRANCH * D_MODEL
RAW_GR = 1792
RAW_AFTER_GR = RAW_GR + GLA_RANK

F32 = jnp.float32
BF16 = jnp.bfloat16


def _bf(t):
    return t.astype(BF16)


def _dot(a, b):
    return jnp.dot(a, b, preferred_element_type=F32)


def _dot_nt(a, b):
    return lax.dot_general(a, b, (((1,), (1,)), ((), ())), preferred_element_type=F32)


def _dot_tn(a, b):
    return lax.dot_general(a, b, (((0,), (0,)), ((), ())), preferred_element_type=F32)


def _rmsnorm(x, g):
    return x * lax.rsqrt(jnp.mean(x * x, axis=-1, keepdims=True) + EPS) * g


def _iota(shape, axis):
    return lax.broadcasted_iota(jnp.int32, shape, axis)


def _head_rmsnorm(o, g):
    n = o.shape[1]
    ones = _bf(jnp.where(_iota((n, n), 0) // HEAD_DV == _iota((n, n), 1) // HEAD_DV, 1.0, 0.0))
    ms = _dot(_bf(o * o), ones) * (1.0 / HEAD_DV)
    return o * lax.rsqrt(ms + EPS) * g


def _log_sigmoid(x):
    return jnp.minimum(x, 0.0) - jnp.log1p(jnp.exp(-jnp.abs(x)))


def _prep_kernel(lre_ref, lim_ref, ldt_ref, bre_ref, bim_ref, lbl_ref, apow_ref, bb_ref, lb_ref):
    depth = lre_ref.shape[0]
    for l in range(depth):
        dt = jnp.exp(ldt_ref[l])
        lr = lre_ref[l]
        li = lim_ref[l]
        mag = jnp.exp(lr * dt)
        ang = li * dt
        ar = mag * jnp.cos(ang)
        ai = mag * jnp.sin(ang)
        den = lr * lr + li * li
        zr = ((ar - 1.0) * lr + ai * li) / den
        zi = (ai * lr - (ar - 1.0) * li) / den
        bre = bre_ref[l]
        bim = bim_ref[l]
        bb_ref[l, 0] = zr * bre - zi * bim
        bb_ref[l, 1] = zr * bim + zi * bre
        pr, pi = ar, ai
        for k in range(N_POW):
            apow_ref[l, 0, k:k + 1, :] = pr
            apow_ref[l, 1, k:k + 1, :] = pi
            pr, pi = pr * pr - pi * pi, pr * pi + pi * pr
    logits = [lbl_ref[l] for l in range(depth)]
    mx = functools.reduce(jnp.maximum, logits)
    ex = [jnp.exp(t - mx) for t in logits]
    tot = functools.reduce(lambda a, b: a + b, ex)
    p = [e / tot for e in ex]
    run = jnp.zeros_like(p[0])
    for l in range(depth):
        run = run + p[l]
        lb_ref[l] = run - p[0]


def _prep(s5_lam_re, s5_lam_im, s5_log_dt, s5_b_re, s5_b_im, hg_lb_logits):
    depth = s5_lam_re.shape[0]
    flat = lambda t: t.reshape(depth, 1, S5_W).astype(F32)
    ldt = jnp.repeat(s5_log_dt.astype(F32), S5_STATE, axis=1).reshape(depth, 1, S5_W)
    bt = lambda t: jnp.transpose(t.astype(F32), (0, 3, 1, 2)).reshape(depth, S5_GROUP, S5_W)
    return pl.pallas_call(
        _prep_kernel,
        out_shape=(jax.ShapeDtypeStruct((depth, 2, N_POW, S5_W), F32),
                   jax.ShapeDtypeStruct((depth, 2, S5_GROUP, S5_W), F32),
                   jax.ShapeDtypeStruct((depth, 1, BRANCH_W), F32)),
        name="prep",
    )(flat(s5_lam_re), flat(s5_lam_im), ldt, bt(s5_b_re), bt(s5_b_im),
      hg_lb_logits.astype(F32).reshape(depth, 1, BRANCH_W))


def _memkv_kernel(mem_ref, g_ref, w_ref, k_ref, v_ref):
    m = mem_ref.shape[1]
    mn = _rmsnorm(mem_ref[0], g_ref[...])
    kv = _dot(_bf(mn), w_ref[...])
    k = kv[:, :BRANCH_W]
    v = kv[:, BRANCH_W:]
    lane_head = _iota((m, BRANCH_W), 1) // XA_HD
    for h in range(HEADS):
        k_ref[0, h * m:(h + 1) * m, :] = _bf(jnp.where(lane_head == h, k, 0.0))
        v_ref[0, h * m:(h + 1) * m, :] = _bf(jnp.where(lane_head == h, v, 0.0))


def _memkv(mem, g, w_kv):
    bsz, m, _ = mem.shape
    out = jax.ShapeDtypeStruct((bsz, HEADS * m, BRANCH_W), BF16)
    return pl.pallas_call(
        _memkv_kernel,
        grid=(bsz,),
        in_specs=[pl.BlockSpec((1, m, D_MODEL), lambda b: (b, 0, 0)),
                  pl.BlockSpec((1, D_MODEL), lambda b: (0, 0)),
                  pl.BlockSpec((D_MODEL, 2 * BRANCH_W), lambda b: (0, 0))],
        out_specs=(pl.BlockSpec((1, HEADS * m, BRANCH_W), lambda b: (b, 0, 0)),
                   pl.BlockSpec((1, HEADS * m, BRANCH_W), lambda b: (b, 0, 0))),
        out_shape=(out, out),
        name="memkv",
    )(mem, g.reshape(1, D_MODEL), _bf(w_kv))


def _level_ref(b, m, row):
    n, w = b.shape
    half = m // 2
    if m == 2:
        return jnp.where((row & 1) == 1, pltpu.roll(b, 1, 0), b)
    if m == 4:
        pos = row & 3
        return jnp.where(pos == 0, pltpu.roll(b, n - 1, 0),
                         jnp.where(pos == 1, b, jnp.where(pos == 2, pltpu.roll(b, 1, 0), pltpu.roll(b, 2, 0))))
    parts = [jnp.broadcast_to(b[k * m + half - 1:k * m + half, :], (m, w)) for k in range(n // m)]
    return parts[0] if len(parts) == 1 else jnp.concatenate(parts, axis=0)


def _lin_attn(q, k, v, lf, st_ref, dk):
    rows, w = q.shape
    wv = v.shape[1]
    nc = HEADS * CHUNK
    row = _iota((CHUNK, w), 0)
    si = _iota((CHUNK, nc), 0)
    sj = _iota((CHUNK, nc), 1) & (CHUNK - 1)
    kmask = (_iota((nc, w), 0) // CHUNK) == (_iota((nc, w), 1) // dk)
    vmask = (_iota((nc, wv), 0) // CHUNK) == (_iota((nc, wv), 1) // HEAD_DV)
    stmask = (_iota((wv, w), 0) // HEAD_DV) == (_iota((wv, w), 1) // dk)
    stack = lambda t: jnp.concatenate([t] * HEADS, axis=0)
    levels = [2 ** i for i in range(1, int(math.log2(CHUNK)) + 1)]
    outs = []
    for c in range(rows // CHUNK):
        sl = slice(c * CHUNK, (c + 1) * CHUNK)
        qc, kc, vc = q[sl], k[sl], v[sl]
        b = lf[sl]
        d = 1
        while d < CHUNK:
            b = b + jnp.where(row >= d, pltpu.roll(b, d, 0), 0.0)
            d *= 2
        s = jnp.where(si == sj, _dot_nt(_bf(qc), _bf(jnp.where(kmask, stack(kc), 0.0))), 0.0)
        for m in levels:
            half = m // 2
            second = (row & (m - 1)) >= half
            ref = _level_ref(b, m, row)
            e = jnp.exp(jnp.where(second, b - ref, ref - b))
            s_m = _dot_nt(_bf(qc * e), _bf(jnp.where(kmask, stack(kc * e), 0.0)))
            pair = ((si & -m) == (sj & -m)) & ((si & (m - 1)) >= half) & ((sj & (m - 1)) < half)
            s = s + jnp.where(pair, s_m, 0.0)
        o = _dot(_bf(s), _bf(jnp.where(vmask, stack(vc), 0.0)))
        st = st_ref[...]
        o = o + _dot_nt(_bf(qc * jnp.exp(b)), _bf(st))
        blast = b[CHUNK - 1:CHUNK, :]
        khat = kc * jnp.exp(blast - b)
        st_ref[...] = st * jnp.exp(blast) + jnp.where(stmask, _dot_tn(_bf(vc), _bf(khat)), 0.0)
        outs.append(o)
    return jnp.concatenate(outs, axis=0)


def _mix_kernel(x_ref, preg_ref, win_ref, apow_ref, bbd_ref, cbd_ref, dskip_ref, wglu_ref, bglu_ref,
                lb_ref, hgng_ref, wg2_ref, bg_ref, glang_ref, kbd_ref, vbd_ref, wbr_ref, wout_ref, postg_ref,
                o_ref, sc_ref, carry_ref, hgst_ref, glast_ref):
    tl = x_ref.shape[1]

    @pl.when(pl.program_id(1) == 0)
    def _():
        sc_ref[:, :, 0:SCAN_PAD, :] = jnp.zeros((2, S5_BLOCKS, SCAN_PAD, LANES), F32)
        carry_ref[...] = jnp.zeros(carry_ref.shape, F32)
        hgst_ref[...] = jnp.zeros(hgst_ref.shape, F32)
        glast_ref[...] = jnp.zeros(glast_ref.shape, F32)

    x = x_ref[0]
    h = _bf(_rmsnorm(x, preg_ref[...]))
    proj = lambda a, n: _dot(h, win_ref[:, a:a + n])

    u = proj(C_S5, BRANCH_W)
    bu = _dot(_bf(u), bbd_ref[...])
    for c in range(2):
        for j in range(S5_BLOCKS):
            lo = c * S5_W + j * LANES
            sc_ref[c, j, SCAN_PAD:SCAN_PAD + tl, :] = bu[:, lo:lo + LANES]
    first_row = _iota((SUBLANES, LANES), 0) == 0

    def scan_block(j, carry):
        apr = apow_ref[0, j]
        api = apow_ref[1, j]
        cr = carry_ref[0, j]
        ci = carry_ref[1, j]
        a1r = apr[0:1]
        a1i = api[0:1]
        top = slice(SCAN_PAD, SCAN_PAD + SUBLANES)
        sc_ref[0, j, top, :] = sc_ref[0, j, top, :] + jnp.where(first_row, a1r * cr - a1i * ci, 0.0)
        sc_ref[1, j, top, :] = sc_ref[1, j, top, :] + jnp.where(first_row, a1r * ci + a1i * cr, 0.0)
        body = slice(SCAN_PAD, SCAN_PAD + tl)
        xr = sc_ref[0, j, body, :]
        xi = sc_ref[1, j, body, :]
        for kk in range(N_POW):
            dd = 1 << kk
            sh = slice(SCAN_PAD - dd, SCAN_PAD - dd + tl)
            sr = sc_ref[0, j, sh, :]
            si = sc_ref[1, j, sh, :]
            akr = apr[kk:kk + 1]
            aki = api[kk:kk + 1]
            xr, xi = xr + akr * sr - aki * si, xi + akr * si + aki * sr
            sc_ref[0, j, body, :] = xr
            sc_ref[1, j, body, :] = xi
        carry_ref[0, j] = xr[tl - 1:tl]
        carry_ref[1, j] = xi[tl - 1:tl]
        return carry

    lax.fori_loop(0, S5_BLOCKS, scan_block, 0)
    s_cat = jnp.concatenate([_bf(sc_ref[c, j, SCAN_PAD:SCAN_PAD + tl, :])
                             for c in range(2) for j in range(S5_BLOCKS)], axis=1)
    y = _dot(s_cat, cbd_ref[...]) + dskip_ref[...] * u
    y = jax.nn.gelu(y)
    y_s5 = y * jax.nn.sigmoid(_dot(_bf(y), wglu_ref[...]) + bglu_ref[...])

    hq = proj(C_HGQ, BRANCH_W)
    hf = proj(C_HGF, BRANCH_W)
    hi = proj(C_HGI, BRANCH_W)
    ho = proj(C_HGO, BRANCH_W)
    lb = lb_ref[...]
    g = lb + (1.0 - lb) * jax.nn.sigmoid(hf)
    o_hg = _lin_attn(hq, (1.0 - lb) * jax.nn.sigmoid(-hf), hi, jnp.log(jnp.maximum(g, GATE_FLOOR)),
                     hgst_ref, HG_DK)
    y_hg = _head_rmsnorm(o_hg, hgng_ref[...]) * jax.nn.sigmoid(ho)

    gq = proj(C_GQ, HEADS * GLA_DK) * (GLA_DK ** -0.5)
    gk = proj(C_GK, HEADS * GLA_DK)
    gv = proj(C_GV, BRANCH_W)
    gr = proj(C_GR, LANES)
    go = proj(C_GO, BRANCH_W)
    lf = _log_sigmoid(_dot(_bf(gr), wg2_ref[...]) + bg_ref[...]) * (1.0 / GLA_GATE_NORM)
    o_gla = _lin_attn(gq, gk, gv, lf, glast_ref, GLA_DK)
    y_gla = _head_rmsnorm(o_gla, glang_ref[...]) * (go * jax.nn.sigmoid(go))

    xq = proj(C_XQ, BRANCH_W) * (XA_HD ** -0.5)
    sc = _dot_nt(_bf(xq), kbd_ref[0])
    m = sc.shape[1] // HEADS
    ps, invs = [], []
    for hh in range(HEADS):
        sh = sc[:, hh * m:(hh + 1) * m]
        p = jnp.exp(sh - jnp.max(sh, axis=-1, keepdims=True))
        ps.append(_bf(p))
        invs.append(1.0 / jnp.sum(p, axis=-1, keepdims=True))
    o_xa = _dot(jnp.concatenate(ps, axis=1), vbd_ref[0])
    lane_head = _iota((tl, BRANCH_W), 1) // XA_HD
    inv = jnp.where(lane_head == 0, invs[0],
                    jnp.where(lane_head == 1, invs[1], jnp.where(lane_head == 2, invs[2], invs[3])))
    y_xa = o_xa * inv

    acc = None
    for kb, yb in enumerate((y_s5, y_hg, y_gla, y_xa)):
        gate = jax.nn.sigmoid(proj(C_GATE + kb * D_MODEL, D_MODEL))
        term = gate * _dot(_bf(yb), wbr_ref[kb])
        acc = term if acc is None else acc + term
    mixed = _dot(_bf(acc), wout_ref[...])
    o_ref[0] = x + _rmsnorm(mixed, postg_ref[...])


def _const_spec(shape):
    return pl.BlockSpec(shape, lambda b, l: (0,) * len(shape), pipeline_mode=pl.Buffered(1))


def _nbytes(shape, dtype):
    return math.prod(shape) * jnp.dtype(dtype).itemsize


def _mix(x, consts, kbd, vbd):
    bsz, seq, _ = x.shape
    tl = ROW_TILE
    x_spec = pl.BlockSpec((1, tl, D_MODEL), lambda b, l: (b, l, 0))
    kv_spec = pl.BlockSpec((1,) + kbd.shape[1:], lambda b, l: (b, 0, 0))
    in_specs = [x_spec] + [_const_spec(c.shape) for c in consts[:13]] + [kv_spec, kv_spec] + \
               [_const_spec(c.shape) for c in consts[13:]]
    scratch = [pltpu.VMEM((2, S5_BLOCKS, SCAN_PAD + tl, LANES), F32),
               pltpu.VMEM((2, S5_BLOCKS, 1, LANES), F32),
               pltpu.VMEM((BRANCH_W, HEADS * HG_DK), F32),
               pltpu.VMEM((BRANCH_W, HEADS * GLA_DK), F32)]
    resident = sum(_nbytes(c.shape, c.dtype) for c in consts) + 4 * _nbytes(kbd.shape[1:], kbd.dtype) \
        + 4 * _nbytes((tl, D_MODEL), F32) + 2 * _nbytes((2, S5_BLOCKS, SCAN_PAD + tl, LANES), F32)
    temporaries = 12 * _nbytes((tl, 2 * S5_W), F32)
    return pl.pallas_call(
        _mix_kernel,
        grid=(bsz, seq // tl),
        in_specs=in_specs,
        out_specs=x_spec,
        out_shape=jax.ShapeDtypeStruct(x.shape, F32),
        scratch_shapes=scratch,
        compiler_params=pltpu.CompilerParams(dimension_semantics=("arbitrary", "arbitrary"),
                                             vmem_limit_bytes=resident + temporaries),
        name="mix",
    )(x, *consts[:13], kbd, vbd, *consts[13:])


def _ffn_kernel(x_ref, preg_ref, wa_ref, wu_ref, cw_ref, cb_ref, wo_ref, postg_ref, o_ref, a_ref):
    tl = x_ref.shape[1]

    @pl.when(pl.program_id(1) == 0)
    def _():
        a_ref[0:SUBLANES, :] = jnp.zeros((SUBLANES, D_FF), F32)

    x = x_ref[0]
    h = _bf(_rmsnorm(x, preg_ref[...]))
    a = _dot(h, wa_ref[...])
    u = _dot(h, wu_ref[...])
    a_ref[SUBLANES:SUBLANES + tl, :] = a
    cw = cw_ref[...]
    conv = (cw[2:3] * a + cw[1:2] * a_ref[SUBLANES - 1:SUBLANES - 1 + tl, :]
            + cw[0:1] * a_ref[SUBLANES - 2:SUBLANES - 2 + tl, :] + cb_ref[...])
    a_ref[0:SUBLANES, :] = a[tl - SUBLANES:tl]
    act = conv * jax.nn.sigmoid(conv) * u
    f = _dot(_bf(act), wo_ref[...])
    o_ref[0] = x + _rmsnorm(f, postg_ref[...])


def _ffn(x, pre_g, w_a, w_u, conv_w, conv_b, w_o, post_g):
    bsz, seq, _ = x.shape
    tl = ROW_TILE
    consts = (pre_g.reshape(1, D_MODEL), w_a, w_u, conv_w, conv_b.reshape(1, D_FF), w_o, post_g.reshape(1, D_MODEL))
    x_spec = pl.BlockSpec((1, tl, D_MODEL), lambda b, l: (b, l, 0))
    resident = sum(_nbytes(c.shape, c.dtype) for c in consts) + 4 * _nbytes((tl, D_MODEL), F32) \
        + _nbytes((SUBLANES + tl, D_FF), F32)
    temporaries = 6 * _nbytes((tl, D_FF), F32)
    return pl.pallas_call(
        _ffn_kernel,
        grid=(bsz, seq // tl),
        in_specs=[x_spec] + [_const_spec(c.shape) for c in consts],
        out_specs=x_spec,
        out_shape=jax.ShapeDtypeStruct(x.shape, F32),
        scratch_shapes=[pltpu.VMEM((SUBLANES + tl, D_FF), F32)],
        compiler_params=pltpu.CompilerParams(dimension_semantics=("arbitrary", "arbitrary"),
                                             vmem_limit_bytes=resident + temporaries),
        name="ffn",
    )(x, *consts)


def _block_diag_rows(t):
    same = (jnp.arange(S5_GROUPS)[:, None, None] == (jnp.arange(S5_W) // S5_STATE)[None, None, :])
    return jnp.where(same, t[None], 0.0).reshape(BRANCH_W, S5_W)


def kernel(x, mem, pre_mix_g, w_in, s5_lam_re, s5_lam_im, s5_log_dt, s5_b_re, s5_b_im, s5_c_re, s5_c_im, s5_d, s5_w_glu, s5_b_glu, hg_lb_logits, hg_norm_g, gla_w_g2, gla_b_g, gla_norm_g, mem_norm_g, w_mem_kv, w_branch, w_out, post_mix_g, pre_ffn_g, w_ffn_in, ffn_conv_w, ffn_conv_b, w_ffn_out, post_ffn_g):
    out_dtype = x.dtype
    bsz, seq, _ = x.shape
    assert seq % ROW_TILE == 0 and x.shape[2] == D_MODEL
    depth = w_in.shape[0]
    x = x.astype(F32)
    mem = mem.astype(F32)
    apow, bb, lower = _prep(s5_lam_re, s5_lam_im, s5_log_dt, s5_b_re, s5_b_im, hg_lb_logits)
    row = lambda t: t.astype(F32).reshape(1, -1)
    for l in range(depth):
        wl = w_in[l]
        w_in_p = _bf(jnp.concatenate([wl[:, :RAW_AFTER_GR], jnp.zeros((D_MODEL, LANES - GLA_RANK), wl.dtype),
                                      wl[:, RAW_AFTER_GR:]], axis=1))
        apow_l = jnp.transpose(apow[l].reshape(2, N_POW, S5_BLOCKS, LANES), (0, 2, 1, 3))
        bbd = _bf(jnp.concatenate([_block_diag_rows(bb[l, 0]), _block_diag_rows(bb[l, 1])], axis=1))
        ct = lambda t: jnp.transpose(t[l].astype(F32), (1, 0, 2)).reshape(S5_GROUP, S5_W)
        cbd = _bf(jnp.concatenate([_block_diag_rows(ct(s5_c_re)).T, -_block_diag_rows(ct(s5_c_im)).T], axis=0))
        wg2 = _bf(jnp.concatenate([gla_w_g2[l], jnp.zeros((LANES - GLA_RANK, HEADS * GLA_DK), gla_w_g2.dtype)], axis=0))
        consts = (row(pre_mix_g[l]), w_in_p, apow_l, bbd, cbd, row(s5_d[l]), _bf(s5_w_glu[l]), row(s5_b_glu[l]),
                  lower[l], row(hg_norm_g[l]), wg2, row(gla_b_g[l]), row(gla_norm_g[l]),
                  _bf(w_branch[l]), _bf(w_out[l]), row(post_mix_g[l]))
        kbd, vbd = _memkv(mem, mem_norm_g[l].astype(F32), w_mem_kv[l])
        x = _mix(x, consts, kbd, vbd)
        x = _ffn(x, pre_ffn_g[l].astype(F32), _bf(w_ffn_in[l, :, :D_FF]), _bf(w_ffn_in[l, :, D_FF:]),
                 ffn_conv_w[l].astype(F32), ffn_conv_b[l].astype(F32), _bf(w_ffn_out[l]), post_ffn_g[l].astype(F32))
    return x.astype(out_dtype)
```

```python
import functools
import math

import jax
import jax.numpy as jnp
from jax import lax
from jax.experimental import pallas as pl
from jax.experimental.pallas import tpu as pltpu

D_MODEL = 1024
N_BRANCH = 4
BRANCH_W = 256
S5_GROUPS = 16
S5_GROUP = 16
S5_STATE = 64
S5_W = S5_GROUPS * S5_STATE
HEADS = 4
HG_DK = 64
GLA_DK = 32
GLA_RANK = 16
GLA_GATE_NORM = 16.0
XA_HD = 64
HEAD_DV = 64
D_FF = 2816
EPS = 1e-6
GATE_FLOOR = 1e-30

LANES = 128
SUBLANES = 8
ROW_TILE = 256
CHUNK = 64
SEG = ROW_TILE // SUBLANES

C_S5 = 0
C_HGQ, C_HGF, C_HGI, C_HGO = 256, 512, 768, 1024
C_GQ, C_GK, C_GV, C_GR, C_GO = 1280, 1408, 1536, 1792, 1920
C_XQ = 2176
C_GATE = 2432
PROJ_PAD = C_GATE + N_BRANCH * D_MODEL
RAW_AFTER_GR = 1792 + GLA_RANK

F32 = jnp.float32
BF16 = jnp.bfloat16


def _bf(t):
    return t.astype(BF16)


def _dot(a, b):
    return jnp.dot(a, b, preferred_element_type=F32)


def _dot_nt(a, b):
    return lax.dot_general(a, b, (((1,), (1,)), ((), ())), preferred_element_type=F32)


def _dot_tn(a, b):
    return lax.dot_general(a, b, (((0,), (0,)), ((), ())), preferred_element_type=F32)


def _rmsnorm(x, g):
    return x * lax.rsqrt(jnp.mean(x * x, axis=-1, keepdims=True) + EPS) * g


def _iota(shape, axis):
    return lax.broadcasted_iota(jnp.int32, shape, axis)


def _head_rmsnorm(o, g):
    n = o.shape[1]
    ones = _bf(jnp.where(_iota((n, n), 0) // HEAD_DV == _iota((n, n), 1) // HEAD_DV, 1.0, 0.0))
    ms = _dot(_bf(o * o), ones) * (1.0 / HEAD_DV)
    return o * lax.rsqrt(ms + EPS) * g


def _log_sigmoid(x):
    return jnp.minimum(x, 0.0) - jnp.log1p(jnp.exp(-jnp.abs(x)))


def _split3(t):
    hi = _bf(t)
    r1 = t - hi.astype(F32)
    mid = _bf(r1)
    return hi, mid, _bf(r1 - mid.astype(F32))


def _nbytes(shape, dtype):
    return math.prod(s for s in shape if s is not None) * jnp.dtype(dtype).itemsize


def _layer_spec(arr, layer, nidx):
    block = (None,) + tuple(arr.shape[1:])
    zeros = (0,) * (arr.ndim - 1)
    if nidx == 1:
        return pl.BlockSpec(block, lambda b: (layer,) + zeros, pipeline_mode=pl.Buffered(1))
    return pl.BlockSpec(block, lambda b, l: (layer,) + zeros, pipeline_mode=pl.Buffered(1))


def _prep_kernel(lre_ref, lim_ref, ldt_ref, bre_ref, bim_ref, lbl_ref, tab_ref, bb_ref, lb_ref):
    depth = lre_ref.shape[0]
    for l in range(depth):
        dt = jnp.exp(ldt_ref[l])
        lr = lre_ref[l]
        li = lim_ref[l]
        mag = jnp.exp(lr * dt)
        ang = li * dt
        ar = mag * jnp.cos(ang)
        ai = mag * jnp.sin(ang)
        den = lr * lr + li * li
        zr = ((ar - 1.0) * lr + ai * li) / den
        zi = (ai * lr - (ar - 1.0) * li) / den
        bre = bre_ref[l]
        bim = bim_ref[l]
        bb_ref[l, 0] = zr * bre - zi * bim
        bb_ref[l, 1] = zr * bim + zi * bre
        pr, pi = ar, ai
        for k in range(SEG):
            rows = slice(k * SUBLANES, (k + 1) * SUBLANES)
            tab_ref[l, 0, rows, :] = jnp.broadcast_to(pr, (SUBLANES, S5_W))
            tab_ref[l, 1, rows, :] = jnp.broadcast_to(pi, (SUBLANES, S5_W))
            pr, pi = pr * ar - pi * ai, pr * ai + pi * ar
    logits = [lbl_ref[l] for l in range(depth)]
    mx = functools.reduce(jnp.maximum, logits)
    ex = [jnp.exp(t - mx) for t in logits]
    tot = functools.reduce(lambda a, b: a + b, ex)
    p = [e / tot for e in ex]
    run = jnp.zeros_like(p[0])
    for l in range(depth):
        run = run + p[l]
        lb_ref[l] = run - p[0]


def _prep(s5_lam_re, s5_lam_im, s5_log_dt, s5_b_re, s5_b_im, hg_lb_logits):
    depth = s5_lam_re.shape[0]
    flat = lambda t: t.reshape(depth, 1, S5_W).astype(F32)
    ldt = jnp.repeat(s5_log_dt.astype(F32), S5_STATE, axis=1).reshape(depth, 1, S5_W)
    bt = lambda t: jnp.transpose(t.astype(F32), (0, 3, 1, 2)).reshape(depth, S5_GROUP, S5_W)
    return pl.pallas_call(
        _prep_kernel,
        out_shape=(jax.ShapeDtypeStruct((depth, 2, ROW_TILE, S5_W), F32),
                   jax.ShapeDtypeStruct((depth, 2, S5_GROUP, S5_W), F32),
                   jax.ShapeDtypeStruct((depth, 1, BRANCH_W), F32)),
        name="prep",
    )(flat(s5_lam_re), flat(s5_lam_im), ldt, bt(s5_b_re), bt(s5_b_im),
      hg_lb_logits.astype(F32).reshape(depth, 1, BRANCH_W))


def _memkv_kernel(mem_ref, g_ref, w_ref, k_ref, v_ref):
    m = mem_ref.shape[0]
    mn = _rmsnorm(mem_ref[...], g_ref[...])
    kv = _dot(_bf(mn), w_ref[...])
    k = kv[:, :BRANCH_W]
    v = kv[:, BRANCH_W:]
    lane_head = _iota((m, BRANCH_W), 1) // XA_HD
    for h in range(HEADS):
        k_ref[h * m:(h + 1) * m, :] = _bf(jnp.where(lane_head == h, k, 0.0))
        v_ref[h * m:(h + 1) * m, :] = _bf(jnp.where(lane_head == h, v, 0.0))


def _memkv(mem, g, w_kv):
    bsz, m, _ = mem.shape
    depth = w_kv.shape[0]
    out = jax.ShapeDtypeStruct((depth, bsz, HEADS * m, BRANCH_W), BF16)
    out_spec = pl.BlockSpec((None, None, HEADS * m, BRANCH_W), lambda l, b: (l, b, 0, 0))
    return pl.pallas_call(
        _memkv_kernel,
        grid=(depth, bsz),
        in_specs=[pl.BlockSpec((None, m, D_MODEL), lambda l, b: (b, 0, 0)),
                  pl.BlockSpec((None, 1, D_MODEL), lambda l, b: (l, 0, 0)),
                  pl.BlockSpec((None, D_MODEL, 2 * BRANCH_W), lambda l, b: (l, 0, 0))],
        out_specs=(out_spec, out_spec),
        out_shape=(out, out),
        name="memkv",
    )(mem, g, w_kv)


def _level_ref(b, m, row):
    n, w = b.shape
    half = m // 2
    if m == 4:
        pos = row & 3
        return jnp.where(pos == 0, pltpu.roll(b, n - 1, 0),
                         jnp.where(pos == 1, b, jnp.where(pos == 2, pltpu.roll(b, 1, 0), pltpu.roll(b, 2, 0))))
    parts = [jnp.broadcast_to(b[k * m + half - 1:k * m + half, :], (m, w)) for k in range(n // m)]
    return parts[0] if len(parts) == 1 else jnp.concatenate(parts, axis=0)


def _lin_attn(q, k, v, lf, st_ref, dk):
    rows, w = q.shape
    wv = v.shape[1]
    nc = HEADS * CHUNK
    nblk = CHUNK // SUBLANES
    row = _iota((CHUNK, w), 0)
    si = _iota((CHUNK, nc), 0)
    sj = _iota((CHUNK, nc), 1) & (CHUNK - 1)
    col_blk = (_iota((SUBLANES, nc), 1) & (CHUNK - 1)) // SUBLANES
    kmask = (_iota((nc, w), 0) // CHUNK) == (_iota((nc, w), 1) // dk)
    vmask = (_iota((nc, wv), 0) // CHUNK) == (_iota((nc, wv), 1) // HEAD_DV)
    stmask = (_iota((wv, w), 0) // HEAD_DV) == (_iota((wv, w), 1) // dk)
    stack = lambda t: jnp.concatenate([t] * HEADS, axis=0)
    levels = [2 ** i for i in range(2, int(math.log2(SUBLANES)) + 1)]
    outs = []
    for c in range(rows // CHUNK):
        sl = slice(c * CHUNK, (c + 1) * CHUNK)
        qc, kc, vc, lfc = q[sl], k[sl], v[sl], lf[sl]
        b = lfc
        d = 1
        while d < CHUNK:
            b = b + jnp.where(row >= d, pltpu.roll(b, d, 0), 0.0)
            d *= 2
        ends = [b[(j + 1) * SUBLANES - 1:(j + 1) * SUBLANES, :] for j in range(nblk)]
        bend = jnp.concatenate([jnp.broadcast_to(e, (SUBLANES, w)) for e in ends], axis=0)
        kx = _bf(jnp.where(kmask, stack(kc * jnp.exp(bend - b)), 0.0))
        qx = jnp.concatenate([qc[(j + 1) * SUBLANES:] * jnp.exp(b[(j + 1) * SUBLANES:] - ends[j])
                              for j in range(nblk - 1)], axis=0)
        cross = _dot_nt(_bf(qx), kx)
        offs = [sum(CHUNK - (t + 1) * SUBLANES for t in range(j)) for j in range(nblk - 1)]
        blocks = [jnp.zeros((SUBLANES, nc), F32)]
        for i in range(1, nblk):
            acc = None
            for j in range(i):
                lo = offs[j] + (i - j - 1) * SUBLANES
                part = jnp.where(col_blk == j, cross[lo:lo + SUBLANES], 0.0)
                acc = part if acc is None else acc + part
            blocks.append(acc)
        s = jnp.concatenate(blocks, axis=0)
        s01 = _dot_nt(_bf(jnp.concatenate([qc, qc * jnp.exp(lfc)], axis=0)), _bf(jnp.where(kmask, stack(kc), 0.0)))
        s = s + jnp.where(si == sj, s01[:CHUNK], 0.0) + jnp.where(((si & 1) == 1) & (sj == si - 1), s01[CHUNK:], 0.0)
        for m in levels:
            half = m // 2
            second = (row & (m - 1)) >= half
            ref = _level_ref(b, m, row)
            e = jnp.exp(jnp.where(second, b - ref, ref - b))
            s_m = _dot_nt(_bf(qc * e), _bf(jnp.where(kmask, stack(kc * e), 0.0)))
            pair = ((si & -m) == (sj & -m)) & ((si & (m - 1)) >= half) & ((sj & (m - 1)) < half)
            s = s + jnp.where(pair, s_m, 0.0)
        o = _dot(_bf(s), _bf(jnp.where(vmask, stack(vc), 0.0)))
        st = st_ref[...]
        o = o + _dot_nt(_bf(qc * jnp.exp(b)), _bf(st))
        blast = b[CHUNK - 1:CHUNK, :]
        khat = kc * jnp.exp(blast - b)
        st_ref[...] = st * jnp.exp(blast) + jnp.where(stmask, _dot_tn(_bf(vc), _bf(khat)), 0.0)
        outs.append(o)
    return jnp.concatenate(outs, axis=0)


def _s5_scan(bu, tab_ref, sc_ref, carry_ref):
    tl = bu.shape[0]
    ar = tab_ref[0, 0:SUBLANES, :]
    ai = tab_ref[1, 0:SUBLANES, :]
    sr = si = None
    for k in range(SEG):
        rows = slice(k * SUBLANES, (k + 1) * SUBLANES)
        br = bu[rows, :S5_W]
        bi = bu[rows, S5_W:]
        if k == 0:
            sr, si = br, bi
        else:
            sr, si = ar * sr - ai * si + br, ar * si + ai * sr + bi
        sc_ref[0, rows, :] = sr
        sc_ref[1, rows, :] = si
    asr = tab_ref[0, tl - 1:tl, :]
    asi = tab_ref[1, tl - 1:tl, :]
    cr = carry_ref[0, 0:1, :]
    ci = carry_ref[1, 0:1, :]
    seg_row = _iota((SUBLANES, S5_W), 0)
    c8r = jnp.broadcast_to(cr, (SUBLANES, S5_W))
    c8i = jnp.broadcast_to(ci, (SUBLANES, S5_W))
    for r in range(1, SUBLANES + 1):
        cr, ci = asr * cr - asi * ci + sr[r - 1:r], asr * ci + asi * cr + si[r - 1:r]
        if r < SUBLANES:
            c8r = jnp.where(seg_row == r, cr, c8r)
            c8i = jnp.where(seg_row == r, ci, c8i)
    carry_ref[0] = jnp.broadcast_to(cr, (SUBLANES, S5_W))
    carry_ref[1] = jnp.broadcast_to(ci, (SUBLANES, S5_W))
    c16r = jnp.concatenate([c8r, c8r], axis=0)
    c16i = jnp.concatenate([c8i, c8i], axis=0)
    pieces = []
    for k2 in range(tl // (2 * SUBLANES)):
        rows = slice(k2 * 2 * SUBLANES, (k2 + 1) * 2 * SUBLANES)
        tr = tab_ref[0, rows, :]
        ti = tab_ref[1, rows, :]
        fr = sc_ref[0, rows, :] + tr * c16r - ti * c16i
        fi = sc_ref[1, rows, :] + tr * c16i + ti * c16r
        pieces.append(jnp.concatenate([_bf(fr), _bf(fi)], axis=1))
    return jnp.concatenate(pieces, axis=0)


def _mix_kernel(x_ref, preg_ref, win_ref, tab_ref, perm_ref, permt_ref, bbd_ref, cbd_ref, dskip_ref, wglu_ref,
                bglu_ref, lb_ref, hgng_ref, wg2_ref, bg_ref, glang_ref, kbd_ref, vbd_ref, wbr_ref, wout_ref,
                postg_ref, o_ref, sc_ref, carry_ref, hgst_ref, glast_ref):
    tl = x_ref.shape[0]

    @pl.when(pl.program_id(1) == 0)
    def _():
        carry_ref[...] = jnp.zeros(carry_ref.shape, F32)
        hgst_ref[...] = jnp.zeros(hgst_ref.shape, F32)
        glast_ref[...] = jnp.zeros(glast_ref.shape, F32)

    x = x_ref[...]
    h = _bf(_rmsnorm(x, preg_ref[...]))
    proj = lambda a, n: _dot(h, win_ref[:, a:a + n])

    u = proj(C_S5, BRANCH_W)
    up = _bf(_dot(perm_ref[...], _bf(u)))
    s_cat = _s5_scan(_dot(up, bbd_ref[...]), tab_ref, sc_ref, carry_ref)
    ysp = _dot(s_cat, cbd_ref[...])
    ys3 = _dot(permt_ref[...], jnp.concatenate(_split3(ysp), axis=1))
    y = ys3[:, :BRANCH_W] + ys3[:, BRANCH_W:2 * BRANCH_W] + ys3[:, 2 * BRANCH_W:] + dskip_ref[...] * u
    y = jax.nn.gelu(y)
    y_s5 = y * jax.nn.sigmoid(_dot(_bf(y), wglu_ref[...]) + bglu_ref[...])

    hq = proj(C_HGQ, BRANCH_W)
    hf = proj(C_HGF, BRANCH_W)
    hi = proj(C_HGI, BRANCH_W)
    ho = proj(C_HGO, BRANCH_W)
    lb = lb_ref[...]
    g = lb + (1.0 - lb) * jax.nn.sigmoid(hf)
    o_hg = _lin_attn(hq, (1.0 - lb) * jax.nn.sigmoid(-hf), hi, jnp.log(jnp.maximum(g, GATE_FLOOR)),
                     hgst_ref, HG_DK)
    y_hg = _head_rmsnorm(o_hg, hgng_ref[...]) * jax.nn.sigmoid(ho)

    gq = proj(C_GQ, HEADS * GLA_DK) * (GLA_DK ** -0.5)
    gk = proj(C_GK, HEADS * GLA_DK)
    gv = proj(C_GV, BRANCH_W)
    gr = proj(C_GR, LANES)
    go = proj(C_GO, BRANCH_W)
    lf = _log_sigmoid(_dot(_bf(gr), wg2_ref[...]) + bg_ref[...]) * (1.0 / GLA_GATE_NORM)
    o_gla = _lin_attn(gq, gk, gv, lf, glast_ref, GLA_DK)
    y_gla = _head_rmsnorm(o_gla, glang_ref[...]) * (go * jax.nn.sigmoid(go))

    xq = proj(C_XQ, BRANCH_W) * (XA_HD ** -0.5)
    sc = _dot_nt(_bf(xq), kbd_ref[...])
    m = sc.shape[1] // HEADS
    ps, invs = [], []
    for hh in range(HEADS):
        sh = sc[:, hh * m:(hh + 1) * m]
        p = jnp.exp(sh - jnp.max(sh, axis=-1, keepdims=True))
        ps.append(_bf(p))
        invs.append(1.0 / jnp.sum(p, axis=-1, keepdims=True))
    o_xa = _dot(jnp.concatenate(ps, axis=1), vbd_ref[...])
    lane_head = _iota((tl, BRANCH_W), 1) // XA_HD
    inv = jnp.where(lane_head == 0, invs[0],
                    jnp.where(lane_head == 1, invs[1], jnp.where(lane_head == 2, invs[2], invs[3])))
    y_xa = o_xa * inv

    acc = None
    for kb, yb in enumerate((y_s5, y_hg, y_gla, y_xa)):
        gate = jax.nn.sigmoid(proj(C_GATE + kb * D_MODEL, D_MODEL))
        term = gate * _dot(_bf(yb), wbr_ref[kb])
        acc = term if acc is None else acc + term
    mixed = _dot(_bf(acc), wout_ref[...])
    o_ref[...] = x + _rmsnorm(mixed, postg_ref[...])


N_MIX_HEAD = 15


def _mix(x, layer, consts, perm, permt, kbd, vbd):
    bsz, seq, _ = x.shape
    tl = ROW_TILE
    x_spec = pl.BlockSpec((None, tl, D_MODEL), lambda b, l: (b, l, 0))
    kv_spec = pl.BlockSpec((None, None) + kbd.shape[2:], lambda b, l: (layer, b, 0, 0))
    perm_spec = pl.BlockSpec(perm.shape, lambda b, l: (0, 0), pipeline_mode=pl.Buffered(1))
    specs = [_layer_spec(c, layer, 2) for c in consts]
    operands = list(consts[:3]) + [perm, permt] + list(consts[3:])
    specs = specs[:3] + [perm_spec, perm_spec] + specs[3:]
    operands = operands[:N_MIX_HEAD] + [kbd, vbd] + operands[N_MIX_HEAD:]
    specs = specs[:N_MIX_HEAD] + [kv_spec, kv_spec] + specs[N_MIX_HEAD:]
    scratch = [pltpu.VMEM((2, tl, S5_W), F32),
               pltpu.VMEM((2, SUBLANES, S5_W), F32),
               pltpu.VMEM((BRANCH_W, HEADS * HG_DK), F32),
               pltpu.VMEM((BRANCH_W, HEADS * GLA_DK), F32)]
    resident = sum(_nbytes(c.shape[1:], c.dtype) for c in consts) + 2 * _nbytes(perm.shape, perm.dtype) \
        + 4 * _nbytes(kbd.shape[2:], kbd.dtype) + 4 * _nbytes((tl, D_MODEL), F32) + _nbytes((2, tl, S5_W), F32)
    temporaries = 12 * _nbytes((tl, 2 * S5_W), F32)
    return pl.pallas_call(
        _mix_kernel,
        grid=(bsz, seq // tl),
        in_specs=[x_spec] + specs,
        out_specs=x_spec,
        out_shape=jax.ShapeDtypeStruct(x.shape, F32),
        scratch_shapes=scratch,
        compiler_params=pltpu.CompilerParams(dimension_semantics=("arbitrary", "arbitrary"),
                                             vmem_limit_bytes=resident + temporaries),
        name="mix",
    )(x, *operands)


def _ffn_kernel(x_ref, preg_ref, wa_ref, wu_ref, cw_ref, cb_ref, wo_ref, postg_ref, o_ref, a_ref):
    tl = x_ref.shape[0]

    @pl.when(pl.program_id(1) == 0)
    def _():
        a_ref[0:SUBLANES, :] = jnp.zeros((SUBLANES, D_FF), F32)

    x = x_ref[...]
    h = _bf(_rmsnorm(x, preg_ref[...]))
    a = _dot(h, wa_ref[...])
    u = _dot(h, wu_ref[...])
    a_ref[SUBLANES:SUBLANES + tl, :] = a
    cw = cw_ref[...]
    conv = (cw[2:3] * a + cw[1:2] * a_ref[SUBLANES - 1:SUBLANES - 1 + tl, :]
            + cw[0:1] * a_ref[SUBLANES - 2:SUBLANES - 2 + tl, :] + cb_ref[...])
    a_ref[0:SUBLANES, :] = a[tl - SUBLANES:tl]
    act = conv * jax.nn.sigmoid(conv) * u
    f = _dot(_bf(act), wo_ref[...])
    o_ref[...] = x + _rmsnorm(f, postg_ref[...])


def _ffn(x, layer, pre_g, w_in, conv_w, conv_b, w_o, post_g):
    bsz, seq, _ = x.shape
    tl = ROW_TILE
    x_spec = pl.BlockSpec((None, tl, D_MODEL), lambda b, l: (b, l, 0))
    half_spec = lambda j: pl.BlockSpec((None, D_MODEL, D_FF), lambda b, l: (layer, 0, j),
                                       pipeline_mode=pl.Buffered(1))
    operands = (pre_g, w_in, w_in, conv_w, conv_b, w_o, post_g)
    specs = [_layer_spec(pre_g, layer, 2), half_spec(0), half_spec(1), _layer_spec(conv_w, layer, 2),
             _layer_spec(conv_b, layer, 2), _layer_spec(w_o, layer, 2), _layer_spec(post_g, layer, 2)]
    resident = sum(_nbytes(c.shape[1:], c.dtype) for c in operands[2:]) + 4 * _nbytes((tl, D_MODEL), F32) \
        + _nbytes((SUBLANES + tl, D_FF), F32)
    temporaries = 6 * _nbytes((tl, D_FF), F32)
    return pl.pallas_call(
        _ffn_kernel,
        grid=(bsz, seq // tl),
        in_specs=[x_spec] + specs,
        out_specs=x_spec,
        out_shape=jax.ShapeDtypeStruct(x.shape, F32),
        scratch_shapes=[pltpu.VMEM((SUBLANES + tl, D_FF), F32)],
        compiler_params=pltpu.CompilerParams(dimension_semantics=("arbitrary", "arbitrary"),
                                             vmem_limit_bytes=resident + temporaries),
        name="ffn",
    )(x, *operands)


def _block_diag_rows(t):
    same = (jnp.arange(S5_GROUPS)[:, None, None] == (jnp.arange(S5_W) // S5_STATE)[None, None, :])
    return jnp.where(same[None], t[:, None], 0.0).reshape(t.shape[0], BRANCH_W, S5_W)


def kernel(x, mem, pre_mix_g, w_in, s5_lam_re, s5_lam_im, s5_log_dt, s5_b_re, s5_b_im, s5_c_re, s5_c_im, s5_d, s5_w_glu, s5_b_glu, hg_lb_logits, hg_norm_g, gla_w_g2, gla_b_g, gla_norm_g, mem_norm_g, w_mem_kv, w_branch, w_out, post_mix_g, pre_ffn_g, w_ffn_in, ffn_conv_w, ffn_conv_b, w_ffn_out, post_ffn_g):
    out_dtype = x.dtype
    bsz, seq, _ = x.shape
    assert seq % ROW_TILE == 0 and x.shape[2] == D_MODEL
    depth = w_in.shape[0]
    x = x.astype(F32)
    rows = lambda t: t.astype(F32).reshape(depth, 1, -1)
    tab, bb, lower = _prep(s5_lam_re, s5_lam_im, s5_log_dt, s5_b_re, s5_b_im, hg_lb_logits)
    kbd, vbd = _memkv(mem.astype(F32), rows(mem_norm_g), _bf(w_mem_kv))
    tok = jnp.arange(ROW_TILE)
    perm = _bf(((tok % SEG) * SUBLANES + tok // SEG)[None, :] == tok[:, None])
    w_in_p = _bf(jnp.concatenate([w_in[:, :, :RAW_AFTER_GR],
                                  jnp.zeros((depth, D_MODEL, LANES - GLA_RANK), w_in.dtype),
                                  w_in[:, :, RAW_AFTER_GR:]], axis=2))
    bbd = _bf(jnp.concatenate([_block_diag_rows(bb[:, 0]), _block_diag_rows(bb[:, 1])], axis=2))
    ct = lambda t: jnp.transpose(t.astype(F32), (0, 2, 1, 3)).reshape(depth, S5_GROUP, S5_W)
    cbd = _bf(jnp.concatenate([jnp.swapaxes(_block_diag_rows(ct(s5_c_re)), 1, 2),
                               -jnp.swapaxes(_block_diag_rows(ct(s5_c_im)), 1, 2)], axis=1))
    wg2 = _bf(jnp.concatenate([gla_w_g2, jnp.zeros((depth, LANES - GLA_RANK, HEADS * GLA_DK), gla_w_g2.dtype)],
                              axis=1))
    mix_consts = (rows(pre_mix_g), w_in_p, tab, bbd, cbd, rows(s5_d), _bf(s5_w_glu), rows(s5_b_glu), lower,
                  rows(hg_norm_g), wg2, rows(gla_b_g), rows(gla_norm_g), _bf(w_branch), _bf(w_out),
                  rows(post_mix_g))
    ffn_consts = (rows(pre_ffn_g), _bf(w_ffn_in), ffn_conv_w.astype(F32), rows(ffn_conv_b), _bf(w_ffn_out),
                  rows(post_ffn_g))
    for layer in range(depth):
        x = _mix(x, layer, mix_consts, perm, perm.T, kbd, vbd)
        x = _ffn(x, layer, *ffn_consts)
    return x.astype(out_dtype)
```

```python
import functools
import math

import jax
import jax.numpy as jnp
from jax import lax
from jax.experimental import pallas as pl
from jax.experimental.pallas import tpu as pltpu

D_MODEL = 1024
N_BRANCH = 4
BRANCH_W = 256
S5_GROUPS = 16
S5_GROUP = 16
S5_STATE = 64
S5_W = S5_GROUPS * S5_STATE
HEADS = 4
HG_DK = 64
GLA_DK = 32
GLA_RANK = 16
GLA_GATE_NORM = 16.0
XA_HD = 64
HEAD_DV = 64
D_FF = 2816
EPS = 1e-6
GATE_FLOOR = 1e-30

LANES = 128
SUBLANES = 8
ROW_TILE = 256
BATCH_TILE = 2
CHUNK = 64
GATE_PIECE = 512
SEG = ROW_TILE // SUBLANES

C_S5 = 0
C_HGQ, C_HGF, C_HGI, C_HGO = 256, 512, 768, 1024
C_GQ, C_GK, C_GV, C_GR, C_GO = 1280, 1408, 1536, 1792, 1920
C_XQ = 2176
C_GATE = 2432
PROJ_PAD = C_GATE + N_BRANCH * D_MODEL
RAW_AFTER_GR = 1792 + GLA_RANK

F32 = jnp.float32
BF16 = jnp.bfloat16


def _bf(t):
    return t.astype(BF16)


def _dot(a, b):
    return jnp.dot(a, b, preferred_element_type=F32)


def _dot_nt(a, b):
    return lax.dot_general(a, b, (((1,), (1,)), ((), ())), preferred_element_type=F32)


def _dot_tn(a, b):
    return lax.dot_general(a, b, (((0,), (0,)), ((), ())), preferred_element_type=F32)


def _rmsnorm(x, g):
    return x * lax.rsqrt(jnp.mean(x * x, axis=-1, keepdims=True) + EPS) * g


def _iota(shape, axis):
    return lax.broadcasted_iota(jnp.int32, shape, axis)


def _head_rmsnorm(o, g):
    n = o.shape[1]
    ones = _bf(jnp.where(_iota((n, n), 0) // HEAD_DV == _iota((n, n), 1) // HEAD_DV, 1.0, 0.0))
    ms = _dot(_bf(o * o), ones) * (1.0 / HEAD_DV)
    return o * lax.rsqrt(ms + EPS) * g


def _log_sigmoid(x):
    return jnp.minimum(x, 0.0) - jnp.log1p(jnp.exp(-jnp.abs(x)))


def _split3(t):
    hi = _bf(t)
    r1 = t - hi.astype(F32)
    mid = _bf(r1)
    return hi, mid, _bf(r1 - mid.astype(F32))


def _nbytes(shape, dtype):
    return math.prod(s for s in shape if s is not None) * jnp.dtype(dtype).itemsize


def _layer_spec(arr, layer):
    block = (None,) + tuple(arr.shape[1:])
    zeros = (0,) * (arr.ndim - 1)
    return pl.BlockSpec(block, lambda b, l: (layer,) + zeros, pipeline_mode=pl.Buffered(1))


def _prep_kernel(lre_ref, lim_ref, ldt_ref, bre_ref, bim_ref, lbl_ref, tab_ref, bb_ref, lb_ref):
    depth = lre_ref.shape[0]
    for l in range(depth):
        dt = jnp.exp(ldt_ref[l])
        lr = lre_ref[l]
        li = lim_ref[l]
        mag = jnp.exp(lr * dt)
        ang = li * dt
        ar = mag * jnp.cos(ang)
        ai = mag * jnp.sin(ang)
        den = lr * lr + li * li
        zr = ((ar - 1.0) * lr + ai * li) / den
        zi = (ai * lr - (ar - 1.0) * li) / den
        bre = bre_ref[l]
        bim = bim_ref[l]
        bb_ref[l, 0] = zr * bre - zi * bim
        bb_ref[l, 1] = zr * bim + zi * bre
        pr, pi = ar, ai
        for k in range(SEG):
            rows = slice(k * SUBLANES, (k + 1) * SUBLANES)
            tab_ref[l, 0, rows, :] = jnp.broadcast_to(pr, (SUBLANES, S5_W))
            tab_ref[l, 1, rows, :] = jnp.broadcast_to(pi, (SUBLANES, S5_W))
            pr, pi = pr * ar - pi * ai, pr * ai + pi * ar
    logits = [lbl_ref[l] for l in range(depth)]
    mx = functools.reduce(jnp.maximum, logits)
    ex = [jnp.exp(t - mx) for t in logits]
    tot = functools.reduce(lambda a, b: a + b, ex)
    p = [e / tot for e in ex]
    run = jnp.zeros_like(p[0])
    for l in range(depth):
        run = run + p[l]
        lb_ref[l] = run - p[0]


def _prep(s5_lam_re, s5_lam_im, s5_log_dt, s5_b_re, s5_b_im, hg_lb_logits):
    depth = s5_lam_re.shape[0]
    flat = lambda t: t.reshape(depth, 1, S5_W).astype(F32)
    ldt = jnp.repeat(s5_log_dt.astype(F32), S5_STATE, axis=1).reshape(depth, 1, S5_W)
    bt = lambda t: jnp.transpose(t.astype(F32), (0, 3, 1, 2)).reshape(depth, S5_GROUP, S5_W)
    return pl.pallas_call(
        _prep_kernel,
        out_shape=(jax.ShapeDtypeStruct((depth, 2, ROW_TILE, S5_W), F32),
                   jax.ShapeDtypeStruct((depth, 2, S5_GROUP, S5_W), F32),
                   jax.ShapeDtypeStruct((depth, 1, BRANCH_W), F32)),
        name="prep",
    )(flat(s5_lam_re), flat(s5_lam_im), ldt, bt(s5_b_re), bt(s5_b_im),
      hg_lb_logits.astype(F32).reshape(depth, 1, BRANCH_W))


def _memkv_kernel(mem_ref, g_ref, w_ref, k_ref, v_ref):
    m = mem_ref.shape[0]
    mn = _rmsnorm(mem_ref[...], g_ref[...])
    kv = _dot(_bf(mn), w_ref[...])
    k = kv[:, :BRANCH_W]
    v = kv[:, BRANCH_W:]
    lane_head = _iota((m, BRANCH_W), 1) // XA_HD
    for h in range(HEADS):
        k_ref[h * m:(h + 1) * m, :] = _bf(jnp.where(lane_head == h, k, 0.0))
        v_ref[h * m:(h + 1) * m, :] = _bf(jnp.where(lane_head == h, v, 0.0))


def _memkv(mem, g, w_kv):
    bsz, m, _ = mem.shape
    depth = w_kv.shape[0]
    out = jax.ShapeDtypeStruct((depth, bsz, HEADS * m, BRANCH_W), BF16)
    out_spec = pl.BlockSpec((None, None, HEADS * m, BRANCH_W), lambda l, b: (l, b, 0, 0))
    return pl.pallas_call(
        _memkv_kernel,
        grid=(depth, bsz),
        in_specs=[pl.BlockSpec((None, m, D_MODEL), lambda l, b: (b, 0, 0)),
                  pl.BlockSpec((None, 1, D_MODEL), lambda l, b: (l, 0, 0)),
                  pl.BlockSpec((None, D_MODEL, 2 * BRANCH_W), lambda l, b: (l, 0, 0))],
        out_specs=(out_spec, out_spec),
        out_shape=(out, out),
        name="memkv",
    )(mem, g, w_kv)


def _level_ref(b, m, row):
    n, w = b.shape
    half = m // 2
    if m == 4:
        pos = row & 3
        return jnp.where(pos == 0, pltpu.roll(b, n - 1, 0),
                         jnp.where(pos == 1, b, jnp.where(pos == 2, pltpu.roll(b, 1, 0), pltpu.roll(b, 2, 0))))
    parts = [jnp.broadcast_to(b[k * m + half - 1:k * m + half, :], (m, w)) for k in range(n // m)]
    return parts[0] if len(parts) == 1 else jnp.concatenate(parts, axis=0)


def _lin_attn(streams, mxu_filler):
    rows = streams[0][0].shape[0]
    wv = HEADS * HEAD_DV
    nc = HEADS * CHUNK
    nblk = CHUNK // SUBLANES
    si = _iota((CHUNK, nc), 0)
    sj = _iota((CHUNK, nc), 1) & (CHUNK - 1)
    col_blk = (_iota((SUBLANES, nc), 1) & (CHUNK - 1)) // SUBLANES
    vmask = (_iota((nc, wv), 0) // CHUNK) == (_iota((nc, wv), 1) // HEAD_DV)
    diag = si == sj
    prev = ((si & 1) == 1) & (sj == si - 1)
    levels = [2 ** i for i in range(2, int(math.log2(SUBLANES)) + 1)]
    pair = {m: ((si & -m) == (sj & -m)) & ((si & (m - 1)) >= m // 2) & ((sj & (m - 1)) < m // 2) for m in levels}
    offs = [sum(CHUNK - (t + 1) * SUBLANES for t in range(j)) for j in range(nblk - 1)]
    stack = lambda t: jnp.concatenate([t] * HEADS, axis=0)
    shape_masks = {}

    def masks(w, dk):
        if (w, dk) not in shape_masks:
            shape_masks[(w, dk)] = (_iota((CHUNK, w), 0),
                                    (_iota((nc, w), 0) // CHUNK) == (_iota((nc, w), 1) // dk),
                                    (_iota((wv, w), 0) // HEAD_DV) == (_iota((wv, w), 1) // dk))
        return shape_masks[(w, dk)]

    outs = [[] for _ in streams]
    for c in range(rows // CHUNK):
        sl = slice(c * CHUNK, (c + 1) * CHUNK)
        staged = []
        for q, k, v, lf, _, _, dk in streams:
            w = q.shape[1]
            row, kmask, _ = masks(w, dk)
            qc, kc, vc, lfc = q[sl], k[sl], v[sl], lf[sl]
            b = lfc
            d = 1
            while d < CHUNK:
                b = b + jnp.where(row >= d, pltpu.roll(b, d, 0), 0.0)
                d *= 2
            ends = [b[(j + 1) * SUBLANES - 1:(j + 1) * SUBLANES, :] for j in range(nblk)]
            bend = jnp.concatenate([jnp.broadcast_to(e, (SUBLANES, w)) for e in ends], axis=0)
            blast = ends[-1]
            ops = {
                "kx": _bf(jnp.where(kmask, stack(kc * jnp.exp(bend - b)), 0.0)),
                "qx": _bf(jnp.concatenate([qc[(j + 1) * SUBLANES:] * jnp.exp(b[(j + 1) * SUBLANES:] - ends[j])
                                           for j in range(nblk - 1)], axis=0)),
                "q01": _bf(jnp.concatenate([qc, qc * jnp.exp(lfc)], axis=0)),
                "k01": _bf(jnp.where(kmask, stack(kc), 0.0)),
                "lv": [],
                "qhat": _bf(qc * jnp.exp(b)),
                "khat": _bf(kc * jnp.exp(blast - b)),
                "vb": _bf(vc),
                "vbd": _bf(jnp.where(vmask, stack(vc), 0.0)),
                "dec": jnp.exp(blast),
            }
            for m in levels:
                second = (row & (m - 1)) >= m // 2
                ref = _level_ref(b, m, row)
                e = jnp.exp(jnp.where(second, b - ref, ref - b))
                ops["lv"].append((_bf(qc * e), _bf(jnp.where(kmask, stack(kc * e), 0.0))))
            staged.append(ops)
        prods = []
        for ops in staged:
            prods.append({"cross": _dot_nt(ops["qx"], ops["kx"]),
                          "s01": _dot_nt(ops["q01"], ops["k01"]),
                          "lv": [_dot_nt(ql, kl) for ql, kl in ops["lv"]],
                          "upd": _dot_tn(ops["vb"], ops["khat"])})
        mxu_filler()
        for idx, (ops, pr) in enumerate(zip(staged, prods)):
            q, _, _, _, st_ref, st_i, dk = streams[idx]
            _, _, stmask = masks(q.shape[1], dk)
            cross = pr["cross"]
            blocks = [jnp.zeros((SUBLANES, nc), F32)]
            for i in range(1, nblk):
                acc = None
                for j in range(i):
                    lo = offs[j] + (i - j - 1) * SUBLANES
                    part = jnp.where(col_blk == j, cross[lo:lo + SUBLANES], 0.0)
                    acc = part if acc is None else acc + part
                blocks.append(acc)
            s = jnp.concatenate(blocks, axis=0)
            s = s + jnp.where(diag, pr["s01"][:CHUNK], 0.0) + jnp.where(prev, pr["s01"][CHUNK:], 0.0)
            for m, s_m in zip(levels, pr["lv"]):
                s = s + jnp.where(pair[m], s_m, 0.0)
            st = st_ref[st_i]
            o = _dot(_bf(s), ops["vbd"]) + _dot_nt(ops["qhat"], _bf(st))
            st_ref[st_i] = st * ops["dec"] + jnp.where(stmask, pr["upd"], 0.0)
            outs[idx].append(o)
    return [jnp.concatenate(o, axis=0) for o in outs]


def _s5_scan(bu, tab_ref, sc_ref, carry_ref):
    tl = bu.shape[0]
    ar = tab_ref[0, 0:SUBLANES, :]
    ai = tab_ref[1, 0:SUBLANES, :]
    sr = si = None
    for k in range(SEG):
        rows = slice(k * SUBLANES, (k + 1) * SUBLANES)
        br = bu[rows, :S5_W]
        bi = bu[rows, S5_W:]
        if k == 0:
            sr, si = br, bi
        else:
            sr, si = ar * sr - ai * si + br, ar * si + ai * sr + bi
        sc_ref[0, rows, :] = sr
        sc_ref[1, rows, :] = si
    asr = tab_ref[0, tl - 1:tl, :]
    asi = tab_ref[1, tl - 1:tl, :]
    cr = carry_ref[0, 0:1, :]
    ci = carry_ref[1, 0:1, :]
    seg_row = _iota((SUBLANES, S5_W), 0)
    c8r = jnp.broadcast_to(cr, (SUBLANES, S5_W))
    c8i = jnp.broadcast_to(ci, (SUBLANES, S5_W))
    for r in range(1, SUBLANES + 1):
        cr, ci = asr * cr - asi * ci + sr[r - 1:r], asr * ci + asi * cr + si[r - 1:r]
        if r < SUBLANES:
            c8r = jnp.where(seg_row == r, cr, c8r)
            c8i = jnp.where(seg_row == r, ci, c8i)
    carry_ref[0] = jnp.broadcast_to(cr, (SUBLANES, S5_W))
    carry_ref[1] = jnp.broadcast_to(ci, (SUBLANES, S5_W))
    c16r = jnp.concatenate([c8r, c8r], axis=0)
    c16i = jnp.concatenate([c8i, c8i], axis=0)
    pieces = []
    for k2 in range(tl // (2 * SUBLANES)):
        rows = slice(k2 * 2 * SUBLANES, (k2 + 1) * 2 * SUBLANES)
        tr = tab_ref[0, rows, :]
        ti = tab_ref[1, rows, :]
        fr = sc_ref[0, rows, :] + tr * c16r - ti * c16i
        fi = sc_ref[1, rows, :] + tr * c16i + ti * c16r
        pieces.append(jnp.concatenate([_bf(fr), _bf(fi)], axis=1))
    return jnp.concatenate(pieces, axis=0)


def _mix_kernel(x_ref, preg_ref, win_ref, tab_ref, perm_ref, permt_ref, bbd_ref, cbd_ref, dskip_ref, wglu_ref,
                bglu_ref, lb_ref, hgng_ref, wg2_ref, bg_ref, glang_ref, kbd_ref, vbd_ref, wbr_ref, wout_ref,
                postg_ref, o_ref, sc_ref, carry_ref, hgst_ref, glast_ref):
    nb, tl, _ = x_ref.shape
    per_batch = [slice(bi * tl, (bi + 1) * tl) for bi in range(nb)]

    @pl.when(pl.program_id(1) == 0)
    def _():
        carry_ref[...] = jnp.zeros(carry_ref.shape, F32)
        hgst_ref[...] = jnp.zeros(hgst_ref.shape, F32)
        glast_ref[...] = jnp.zeros(glast_ref.shape, F32)

    hs = [_bf(_rmsnorm(x_ref[bi], preg_ref[...])) for bi in range(nb)]

    gate_logits = {}
    pending = [(kb, n, bi) for kb in range(N_BRANCH) for n in range(D_MODEL // GATE_PIECE) for bi in range(nb)]

    def fill(count):
        for _ in range(min(count, len(pending))):
            kb, n, bi = pending.pop(0)
            lo = C_GATE + kb * D_MODEL + n * GATE_PIECE
            gate_logits[(kb, n, bi)] = _dot(hs[bi], win_ref[:, lo:lo + GATE_PIECE])

    pin = jnp.concatenate([_dot(hb, win_ref[:, 0:C_GATE]) for hb in hs], axis=0)
    cols = lambda a, n: pin[:, a:a + n]
    u = cols(C_S5, BRANCH_W)
    ub = _bf(u)
    bus = [_dot(_bf(_dot(perm_ref[...], ub[r])), bbd_ref[...]) for r in per_batch]
    y_s5 = []
    for bi, r in enumerate(per_batch):
        fill(2)
        s_cat = _s5_scan(bus[bi], tab_ref, sc_ref.at[bi], carry_ref.at[bi])
        ysp = _dot(s_cat[:, :S5_W], cbd_ref[0:S5_W, :]) + _dot(s_cat[:, S5_W:], cbd_ref[S5_W:, :])
        ys3 = _dot(permt_ref[...], jnp.concatenate(_split3(ysp), axis=1))
        y = ys3[:, :BRANCH_W] + ys3[:, BRANCH_W:2 * BRANCH_W] + ys3[:, 2 * BRANCH_W:] + dskip_ref[...] * u[r]
        y = jax.nn.gelu(y)
        y_s5.append(y * jax.nn.sigmoid(_dot(_bf(y), wglu_ref[...]) + bglu_ref[...]))
    y_s5 = jnp.concatenate(y_s5, axis=0)

    hq = cols(C_HGQ, BRANCH_W)
    hf = cols(C_HGF, BRANCH_W)
    hi = cols(C_HGI, BRANCH_W)
    ho = cols(C_HGO, BRANCH_W)
    lb = lb_ref[...]
    hk = (1.0 - lb) * jax.nn.sigmoid(-hf)
    hlf = jnp.log(jnp.maximum(lb + (1.0 - lb) * jax.nn.sigmoid(hf), GATE_FLOOR))
    gq = cols(C_GQ, HEADS * GLA_DK) * (GLA_DK ** -0.5)
    gk = cols(C_GK, HEADS * GLA_DK)
    gv = cols(C_GV, BRANCH_W)
    gr = cols(C_GR, LANES)
    go = cols(C_GO, BRANCH_W)
    glf = _log_sigmoid(_dot(_bf(gr), wg2_ref[...]) + bg_ref[...]) * (1.0 / GLA_GATE_NORM)
    streams = [(hq[r], hk[r], hi[r], hlf[r], hgst_ref, bi, HG_DK) for bi, r in enumerate(per_batch)] + \
              [(gq[r], gk[r], gv[r], glf[r], glast_ref, bi, GLA_DK) for bi, r in enumerate(per_batch)]
    outs = _lin_attn(streams, functools.partial(fill, 3))
    y_hg = jnp.concatenate([_head_rmsnorm(o, hgng_ref[...]) for o in outs[:nb]], axis=0) * jax.nn.sigmoid(ho)
    y_gla = jnp.concatenate([_head_rmsnorm(o, glang_ref[...]) for o in outs[nb:]], axis=0) \
        * (go * jax.nn.sigmoid(go))

    xq = _bf(cols(C_XQ, BRANCH_W) * (XA_HD ** -0.5))
    lane_head = _iota((tl, BRANCH_W), 1) // XA_HD
    y_xa = []
    for bi, r in enumerate(per_batch):
        sc = _dot_nt(xq[r], kbd_ref[bi])
        m = sc.shape[1] // HEADS
        ps, invs = [], []
        for hh in range(HEADS):
            sh = sc[:, hh * m:(hh + 1) * m]
            p = jnp.exp(sh - jnp.max(sh, axis=-1, keepdims=True))
            ps.append(_bf(p))
            invs.append(1.0 / jnp.sum(p, axis=-1, keepdims=True))
        o_xa = _dot(jnp.concatenate(ps, axis=1), vbd_ref[bi])
        inv = jnp.where(lane_head == 0, invs[0],
                        jnp.where(lane_head == 1, invs[1], jnp.where(lane_head == 2, invs[2], invs[3])))
        y_xa.append(o_xa * inv)
    y_xa = jnp.concatenate(y_xa, axis=0)

    fill(len(pending))
    for bi, r in enumerate(per_batch):
        acc = None
        for kb, yb in enumerate((y_s5, y_hg, y_gla, y_xa)):
            gate = jax.nn.sigmoid(jnp.concatenate([gate_logits[(kb, n, bi)]
                                                   for n in range(D_MODEL // GATE_PIECE)], axis=1))
            term = gate * _dot(_bf(yb[r]), wbr_ref[kb])
            acc = term if acc is None else acc + term
        mixed = _dot(_bf(acc), wout_ref[...])
        o_ref[bi] = x_ref[bi] + _rmsnorm(mixed, postg_ref[...])


N_MIX_HEAD = 15


def _mix(x, layer, consts, perm, permt, kbd, vbd):
    bsz, seq, _ = x.shape
    tl, nb = ROW_TILE, BATCH_TILE
    x_spec = pl.BlockSpec((nb, tl, D_MODEL), lambda b, l: (b, l, 0))
    kv_spec = pl.BlockSpec((None, nb) + kbd.shape[2:], lambda b, l: (layer, b, 0, 0))
    perm_spec = pl.BlockSpec(perm.shape, lambda b, l: (0, 0), pipeline_mode=pl.Buffered(1))
    specs = [_layer_spec(c, layer) for c in consts]
    operands = list(consts[:3]) + [perm, permt] + list(consts[3:])
    specs = specs[:3] + [perm_spec, perm_spec] + specs[3:]
    operands = operands[:N_MIX_HEAD] + [kbd, vbd] + operands[N_MIX_HEAD:]
    specs = specs[:N_MIX_HEAD] + [kv_spec, kv_spec] + specs[N_MIX_HEAD:]
    scratch = [pltpu.VMEM((nb, 2, tl, S5_W), F32),
               pltpu.VMEM((nb, 2, SUBLANES, S5_W), F32),
               pltpu.VMEM((nb, BRANCH_W, HEADS * HG_DK), F32),
               pltpu.VMEM((nb, BRANCH_W, HEADS * GLA_DK), F32)]
    resident = sum(_nbytes(c.shape[1:], c.dtype) for c in consts) + 2 * _nbytes(perm.shape, perm.dtype) \
        + nb * (4 * _nbytes(kbd.shape[2:], kbd.dtype) + 4 * _nbytes((tl, D_MODEL), F32) + _nbytes((2, tl, S5_W), F32))
    temporaries = nb * 6 * _nbytes((tl, 2 * S5_W), F32)
    return pl.pallas_call(
        _mix_kernel,
        grid=(bsz // nb, seq // tl),
        in_specs=[x_spec] + specs,
        out_specs=x_spec,
        out_shape=jax.ShapeDtypeStruct(x.shape, F32),
        scratch_shapes=scratch,
        compiler_params=pltpu.CompilerParams(dimension_semantics=("arbitrary", "arbitrary"),
                                             vmem_limit_bytes=resident + temporaries),
        name="mix",
    )(x, *operands)


def _ffn_kernel(x_ref, preg_ref, wa_ref, wu_ref, cw_ref, cb_ref, wo_ref, postg_ref, o_ref, a_ref):
    tl = x_ref.shape[0]

    @pl.when(pl.program_id(1) == 0)
    def _():
        a_ref[0:SUBLANES, :] = jnp.zeros((SUBLANES, D_FF), F32)

    x = x_ref[...]
    h = _bf(_rmsnorm(x, preg_ref[...]))
    a = _dot(h, wa_ref[...])
    u = _dot(h, wu_ref[...])
    a_ref[SUBLANES:SUBLANES + tl, :] = a
    cw = cw_ref[...]
    conv = (cw[2:3] * a + cw[1:2] * a_ref[SUBLANES - 1:SUBLANES - 1 + tl, :]
            + cw[0:1] * a_ref[SUBLANES - 2:SUBLANES - 2 + tl, :] + cb_ref[...])
    a_ref[0:SUBLANES, :] = a[tl - SUBLANES:tl]
    act = conv * jax.nn.sigmoid(conv) * u
    f = _dot(_bf(act), wo_ref[...])
    o_ref[...] = x + _rmsnorm(f, postg_ref[...])


def _ffn(x, layer, pre_g, w_in, conv_w, conv_b, w_o, post_g):
    bsz, seq, _ = x.shape
    tl = ROW_TILE
    x_spec = pl.BlockSpec((None, tl, D_MODEL), lambda b, l: (b, l, 0))
    half_spec = lambda j: pl.BlockSpec((None, D_MODEL, D_FF), lambda b, l: (layer, 0, j),
                                       pipeline_mode=pl.Buffered(1))
    operands = (pre_g, w_in, w_in, conv_w, conv_b, w_o, post_g)
    specs = [_layer_spec(pre_g, layer), half_spec(0), half_spec(1), _layer_spec(conv_w, layer),
             _layer_spec(conv_b, layer), _layer_spec(w_o, layer), _layer_spec(post_g, layer)]
    resident = sum(_nbytes(c.shape[1:], c.dtype) for c in operands[2:]) + 4 * _nbytes((tl, D_MODEL), F32) \
        + _nbytes((SUBLANES + tl, D_FF), F32)
    temporaries = 6 * _nbytes((tl, D_FF), F32)
    return pl.pallas_call(
        _ffn_kernel,
        grid=(bsz, seq // tl),
        in_specs=[x_spec] + specs,
        out_specs=x_spec,
        out_shape=jax.ShapeDtypeStruct(x.shape, F32),
        scratch_shapes=[pltpu.VMEM((SUBLANES + tl, D_FF), F32)],
        compiler_params=pltpu.CompilerParams(dimension_semantics=("arbitrary", "arbitrary"),
                                             vmem_limit_bytes=resident + temporaries),
        name="ffn",
    )(x, *operands)


def _block_diag_rows(t):
    same = (jnp.arange(S5_GROUPS)[:, None, None] == (jnp.arange(S5_W) // S5_STATE)[None, None, :])
    return jnp.where(same[None], t[:, None], 0.0).reshape(t.shape[0], BRANCH_W, S5_W)


def kernel(x, mem, pre_mix_g, w_in, s5_lam_re, s5_lam_im, s5_log_dt, s5_b_re, s5_b_im, s5_c_re, s5_c_im, s5_d, s5_w_glu, s5_b_glu, hg_lb_logits, hg_norm_g, gla_w_g2, gla_b_g, gla_norm_g, mem_norm_g, w_mem_kv, w_branch, w_out, post_mix_g, pre_ffn_g, w_ffn_in, ffn_conv_w, ffn_conv_b, w_ffn_out, post_ffn_g):
    out_dtype = x.dtype
    bsz, seq, _ = x.shape
    assert seq % ROW_TILE == 0 and bsz % BATCH_TILE == 0 and x.shape[2] == D_MODEL
    depth = w_in.shape[0]
    x = x.astype(F32)
    rows = lambda t: t.astype(F32).reshape(depth, 1, -1)
    tab, bb, lower = _prep(s5_lam_re, s5_lam_im, s5_log_dt, s5_b_re, s5_b_im, hg_lb_logits)
    kbd, vbd = _memkv(mem.astype(F32), rows(mem_norm_g), _bf(w_mem_kv))
    tok = jnp.arange(ROW_TILE)
    perm = _bf(((tok % SEG) * SUBLANES + tok // SEG)[None, :] == tok[:, None])
    w_in_p = _bf(jnp.concatenate([w_in[:, :, :RAW_AFTER_GR],
                                  jnp.zeros((depth, D_MODEL, LANES - GLA_RANK), w_in.dtype),
                                  w_in[:, :, RAW_AFTER_GR:]], axis=2))
    bbd = _bf(jnp.concatenate([_block_diag_rows(bb[:, 0]), _block_diag_rows(bb[:, 1])], axis=2))
    ct = lambda t: jnp.transpose(t.astype(F32), (0, 2, 1, 3)).reshape(depth, S5_GROUP, S5_W)
    cbd = _bf(jnp.concatenate([jnp.swapaxes(_block_diag_rows(ct(s5_c_re)), 1, 2),
                               -jnp.swapaxes(_block_diag_rows(ct(s5_c_im)), 1, 2)], axis=1))
    wg2 = _bf(jnp.concatenate([gla_w_g2, jnp.zeros((depth, LANES - GLA_RANK, HEADS * GLA_DK), gla_w_g2.dtype)],
                              axis=1))
    mix_consts = (rows(pre_mix_g), w_in_p, tab, bbd, cbd, rows(s5_d), _bf(s5_w_glu), rows(s5_b_glu), lower,
                  rows(hg_norm_g), wg2, rows(gla_b_g), rows(gla_norm_g), _bf(w_branch), _bf(w_out),
                  rows(post_mix_g))
    ffn_consts = (rows(pre_ffn_g), _bf(w_ffn_in), ffn_conv_w.astype(F32), rows(ffn_conv_b), _bf(w_ffn_out),
                  rows(post_ffn_g))
    for layer in range(depth):
        x = _mix(x, layer, mix_consts, perm, perm.T, kbd, vbd)
        x = _ffn(x, layer, *ffn_consts)
    return x.astype(out_dtype)
```

```python
import functools
import math

import jax
import jax.numpy as jnp
from jax import lax
from jax.experimental import pallas as pl
from jax.experimental.pallas import tpu as pltpu

D_MODEL = 1024
N_BRANCH = 4
BRANCH_W = 256
S5_GROUPS = 16
S5_GROUP = 16
S5_STATE = 64
S5_W = S5_GROUPS * S5_STATE
HEADS = 4
HG_DK = 64
GLA_DK = 32
GLA_RANK = 16
GLA_GATE_NORM = 16.0
XA_HD = 64
HEAD_DV = 64
D_FF = 2816
EPS = 1e-6
GATE_FLOOR = 1e-30

LANES = 128
SUBLANES = 8
ROW_TILE = 256
BATCH_TILE = 2
CHUNK = 64
GATE_PIECE = 512
SEG = ROW_TILE // SUBLANES

C_S5 = 0
C_HGQ, C_HGF, C_HGI, C_HGO = 256, 512, 768, 1024
C_GQ, C_GK, C_GV, C_GR, C_GO = 1280, 1408, 1536, 1792, 1920
C_XQ = 2176
C_GATE = 2432
PROJ_PAD = C_GATE + N_BRANCH * D_MODEL
RAW_AFTER_GR = 1792 + GLA_RANK

F32 = jnp.float32
BF16 = jnp.bfloat16


def _bf(t):
    return t.astype(BF16)


def _dot(a, b):
    return jnp.dot(a, b, preferred_element_type=F32)


def _dot_nt(a, b):
    return lax.dot_general(a, b, (((1,), (1,)), ((), ())), preferred_element_type=F32)


def _dot_tn(a, b):
    return lax.dot_general(a, b, (((0,), (0,)), ((), ())), preferred_element_type=F32)


def _rmsnorm(x, g):
    return x * lax.rsqrt(jnp.mean(x * x, axis=-1, keepdims=True) + EPS) * g


def _iota(shape, axis):
    return lax.broadcasted_iota(jnp.int32, shape, axis)


def _head_rmsnorm(o, g):
    n = o.shape[1]
    ones = _bf(jnp.where(_iota((n, n), 0) // HEAD_DV == _iota((n, n), 1) // HEAD_DV, 1.0, 0.0))
    ms = _dot(_bf(o * o), ones) * (1.0 / HEAD_DV)
    return o * lax.rsqrt(ms + EPS) * g


def _log_sigmoid(x):
    return jnp.minimum(x, 0.0) - jnp.log1p(jnp.exp(-jnp.abs(x)))


def _split3(t):
    hi = _bf(t)
    r1 = t - hi.astype(F32)
    mid = _bf(r1)
    return hi, mid, _bf(r1 - mid.astype(F32))


def _nbytes(shape, dtype):
    return math.prod(s for s in shape if s is not None) * jnp.dtype(dtype).itemsize


def _layer_spec(arr, layer):
    block = (None,) + tuple(arr.shape[1:])
    zeros = (0,) * (arr.ndim - 1)
    return pl.BlockSpec(block, lambda b, l: (layer,) + zeros, pipeline_mode=pl.Buffered(1))


def _prep_kernel(lre_ref, lim_ref, ldt_ref, bre_ref, bim_ref, lbl_ref, tab_ref, bb_ref, lb_ref):
    depth = lre_ref.shape[0]
    for l in range(depth):
        dt = jnp.exp(ldt_ref[l])
        lr = lre_ref[l]
        li = lim_ref[l]
        mag = jnp.exp(lr * dt)
        ang = li * dt
        ar = mag * jnp.cos(ang)
        ai = mag * jnp.sin(ang)
        den = lr * lr + li * li
        zr = ((ar - 1.0) * lr + ai * li) / den
        zi = (ai * lr - (ar - 1.0) * li) / den
        bre = bre_ref[l]
        bim = bim_ref[l]
        bb_ref[l, 0] = zr * bre - zi * bim
        bb_ref[l, 1] = zr * bim + zi * bre
        pr, pi = ar, ai
        for k in range(SEG):
            rows = slice(k * SUBLANES, (k + 1) * SUBLANES)
            tab_ref[l, 0, rows, :] = jnp.broadcast_to(pr, (SUBLANES, S5_W))
            tab_ref[l, 1, rows, :] = jnp.broadcast_to(pi, (SUBLANES, S5_W))
            pr, pi = pr * ar - pi * ai, pr * ai + pi * ar
    logits = [lbl_ref[l] for l in range(depth)]
    mx = functools.reduce(jnp.maximum, logits)
    ex = [jnp.exp(t - mx) for t in logits]
    tot = functools.reduce(lambda a, b: a + b, ex)
    p = [e / tot for e in ex]
    run = jnp.zeros_like(p[0])
    for l in range(depth):
        run = run + p[l]
        lb_ref[l] = run - p[0]


def _prep(s5_lam_re, s5_lam_im, s5_log_dt, s5_b_re, s5_b_im, hg_lb_logits):
    depth = s5_lam_re.shape[0]
    flat = lambda t: t.reshape(depth, 1, S5_W).astype(F32)
    ldt = jnp.repeat(s5_log_dt.astype(F32), S5_STATE, axis=1).reshape(depth, 1, S5_W)
    bt = lambda t: jnp.transpose(t.astype(F32), (0, 3, 1, 2)).reshape(depth, S5_GROUP, S5_W)
    return pl.pallas_call(
        _prep_kernel,
        out_shape=(jax.ShapeDtypeStruct((depth, 2, ROW_TILE, S5_W), F32),
                   jax.ShapeDtypeStruct((depth, 2, S5_GROUP, S5_W), F32),
                   jax.ShapeDtypeStruct((depth, 1, BRANCH_W), F32)),
        name="prep",
    )(flat(s5_lam_re), flat(s5_lam_im), ldt, bt(s5_b_re), bt(s5_b_im),
      hg_lb_logits.astype(F32).reshape(depth, 1, BRANCH_W))


def _memkv_kernel(mem_ref, g_ref, w_ref, k_ref, v_ref):
    m = mem_ref.shape[0]
    mn = _rmsnorm(mem_ref[...], g_ref[...])
    kv = _dot(_bf(mn), w_ref[...])
    k = kv[:, :BRANCH_W]
    v = kv[:, BRANCH_W:]
    lane_head = _iota((m, BRANCH_W), 1) // XA_HD
    for h in range(HEADS):
        k_ref[h * m:(h + 1) * m, :] = _bf(jnp.where(lane_head == h, k, 0.0))
        v_ref[h * m:(h + 1) * m, :] = _bf(jnp.where(lane_head == h, v, 0.0))


def _memkv(mem, g, w_kv):
    bsz, m, _ = mem.shape
    depth = w_kv.shape[0]
    out = jax.ShapeDtypeStruct((depth, bsz, HEADS * m, BRANCH_W), BF16)
    out_spec = pl.BlockSpec((None, None, HEADS * m, BRANCH_W), lambda l, b: (l, b, 0, 0))
    return pl.pallas_call(
        _memkv_kernel,
        grid=(depth, bsz),
        in_specs=[pl.BlockSpec((None, m, D_MODEL), lambda l, b: (b, 0, 0)),
                  pl.BlockSpec((None, 1, D_MODEL), lambda l, b: (l, 0, 0)),
                  pl.BlockSpec((None, D_MODEL, 2 * BRANCH_W), lambda l, b: (l, 0, 0))],
        out_specs=(out_spec, out_spec),
        out_shape=(out, out),
        name="memkv",
    )(mem, g, w_kv)


def _level_ref(b, m, row):
    n, w = b.shape
    half = m // 2
    if m == 4:
        pos = row & 3
        return jnp.where(pos == 0, pltpu.roll(b, n - 1, 0),
                         jnp.where(pos == 1, b, jnp.where(pos == 2, pltpu.roll(b, 1, 0), pltpu.roll(b, 2, 0))))
    parts = [jnp.broadcast_to(b[k * m + half - 1:k * m + half, :], (m, w)) for k in range(n // m)]
    return parts[0] if len(parts) == 1 else jnp.concatenate(parts, axis=0)


def _lin_attn(streams, mxu_filler):
    rows = streams[0][0].shape[0]
    wv = HEADS * HEAD_DV
    nc = HEADS * CHUNK
    nblk = CHUNK // SUBLANES
    si = _iota((CHUNK, nc), 0)
    sj = _iota((CHUNK, nc), 1) & (CHUNK - 1)
    col_blk = (_iota((SUBLANES, nc), 1) & (CHUNK - 1)) // SUBLANES
    vmask = (_iota((nc, wv), 0) // CHUNK) == (_iota((nc, wv), 1) // HEAD_DV)
    diag = si == sj
    prev = ((si & 1) == 1) & (sj == si - 1)
    levels = [2 ** i for i in range(2, int(math.log2(SUBLANES)) + 1)]
    pair = {m: ((si & -m) == (sj & -m)) & ((si & (m - 1)) >= m // 2) & ((sj & (m - 1)) < m // 2) for m in levels}
    offs = [sum(CHUNK - (t + 1) * SUBLANES for t in range(j)) for j in range(nblk - 1)]
    stack = lambda t: jnp.concatenate([t] * HEADS, axis=0)
    shape_masks = {}

    def masks(w, dk):
        if (w, dk) not in shape_masks:
            shape_masks[(w, dk)] = (_iota((CHUNK, w), 0),
                                    (_iota((nc, w), 0) // CHUNK) == (_iota((nc, w), 1) // dk),
                                    (_iota((wv, w), 0) // HEAD_DV) == (_iota((wv, w), 1) // dk))
        return shape_masks[(w, dk)]

    outs = [[] for _ in streams]
    for c in range(rows // CHUNK):
        sl = slice(c * CHUNK, (c + 1) * CHUNK)
        staged = []
        for q, k, v, lf, _, _, dk in streams:
            w = q.shape[1]
            row, kmask, _ = masks(w, dk)
            qc, kc, vc, lfc = q[sl], k[sl], v[sl], lf[sl]
            b = lfc
            d = 1
            while d < CHUNK:
                b = b + jnp.where(row >= d, pltpu.roll(b, d, 0), 0.0)
                d *= 2
            ends = [b[(j + 1) * SUBLANES - 1:(j + 1) * SUBLANES, :] for j in range(nblk)]
            bend = jnp.concatenate([jnp.broadcast_to(e, (SUBLANES, w)) for e in ends], axis=0)
            blast = ends[-1]
            ops = {
                "kx": _bf(jnp.where(kmask, stack(kc * jnp.exp(bend - b)), 0.0)),
                "qx": _bf(jnp.concatenate([qc[(j + 1) * SUBLANES:] * jnp.exp(b[(j + 1) * SUBLANES:] - ends[j])
                                           for j in range(nblk - 1)], axis=0)),
                "q01": _bf(jnp.concatenate([qc, qc * jnp.exp(lfc)], axis=0)),
                "k01": _bf(jnp.where(kmask, stack(kc), 0.0)),
                "lv": [],
                "qhat": _bf(qc * jnp.exp(b)),
                "khat": _bf(kc * jnp.exp(blast - b)),
                "vb": _bf(vc),
                "vbd": _bf(jnp.where(vmask, stack(vc), 0.0)),
                "dec": jnp.exp(blast),
            }
            for m in levels:
                second = (row & (m - 1)) >= m // 2
                ref = _level_ref(b, m, row)
                e = jnp.exp(jnp.where(second, b - ref, ref - b))
                ops["lv"].append((_bf(qc * e), _bf(jnp.where(kmask, stack(kc * e), 0.0))))
            staged.append(ops)
        prods = []
        for ops in staged:
            prods.append({"cross": _dot_nt(ops["qx"], ops["kx"]),
                          "s01": _dot_nt(ops["q01"], ops["k01"]),
                          "lv": [_dot_nt(ql, kl) for ql, kl in ops["lv"]],
                          "upd": _dot_tn(ops["vb"], ops["khat"])})
        mxu_filler()
        for idx, (ops, pr) in enumerate(zip(staged, prods)):
            q, _, _, _, st_ref, st_i, dk = streams[idx]
            _, _, stmask = masks(q.shape[1], dk)
            cross = pr["cross"]
            blocks = [jnp.zeros((SUBLANES, nc), F32)]
            for i in range(1, nblk):
                acc = None
                for j in range(i):
                    lo = offs[j] + (i - j - 1) * SUBLANES
                    part = jnp.where(col_blk == j, cross[lo:lo + SUBLANES], 0.0)
                    acc = part if acc is None else acc + part
                blocks.append(acc)
            s = jnp.concatenate(blocks, axis=0)
            s = s + jnp.where(diag, pr["s01"][:CHUNK], 0.0) + jnp.where(prev, pr["s01"][CHUNK:], 0.0)
            for m, s_m in zip(levels, pr["lv"]):
                s = s + jnp.where(pair[m], s_m, 0.0)
            st = st_ref[st_i]
            o = _dot(_bf(s), ops["vbd"]) + _dot_nt(ops["qhat"], _bf(st))
            st_ref[st_i] = st * ops["dec"] + jnp.where(stmask, pr["upd"], 0.0)
            outs[idx].append(o)
    return [jnp.concatenate(o, axis=0) for o in outs]


def _s5_scan(bu, tab_ref, sc_ref, carry_ref):
    tl = bu.shape[0]
    ar = tab_ref[0, 0:SUBLANES, :]
    ai = tab_ref[1, 0:SUBLANES, :]
    sr = si = None
    for k in range(SEG):
        rows = slice(k * SUBLANES, (k + 1) * SUBLANES)
        br = bu[rows, :S5_W]
        bi = bu[rows, S5_W:]
        if k == 0:
            sr, si = br, bi
        else:
            sr, si = ar * sr - ai * si + br, ar * si + ai * sr + bi
        sc_ref[0, rows, :] = sr
        sc_ref[1, rows, :] = si
    asr = tab_ref[0, tl - 1:tl, :]
    asi = tab_ref[1, tl - 1:tl, :]
    cr = carry_ref[0, 0:1, :]
    ci = carry_ref[1, 0:1, :]
    seg_row = _iota((SUBLANES, S5_W), 0)
    c8r = jnp.broadcast_to(cr, (SUBLANES, S5_W))
    c8i = jnp.broadcast_to(ci, (SUBLANES, S5_W))
    for r in range(1, SUBLANES + 1):
        cr, ci = asr * cr - asi * ci + sr[r - 1:r], asr * ci + asi * cr + si[r - 1:r]
        if r < SUBLANES:
            c8r = jnp.where(seg_row == r, cr, c8r)
            c8i = jnp.where(seg_row == r, ci, c8i)
    carry_ref[0] = jnp.broadcast_to(cr, (SUBLANES, S5_W))
    carry_ref[1] = jnp.broadcast_to(ci, (SUBLANES, S5_W))
    c16r = jnp.concatenate([c8r, c8r], axis=0)
    c16i = jnp.concatenate([c8i, c8i], axis=0)
    pieces = []
    for k2 in range(tl // (2 * SUBLANES)):
        rows = slice(k2 * 2 * SUBLANES, (k2 + 1) * 2 * SUBLANES)
        tr = tab_ref[0, rows, :]
        ti = tab_ref[1, rows, :]
        fr = sc_ref[0, rows, :] + tr * c16r - ti * c16i
        fi = sc_ref[1, rows, :] + tr * c16i + ti * c16r
        pieces.append(jnp.concatenate([_bf(fr), _bf(fi)], axis=1))
    return jnp.concatenate(pieces, axis=0)


def _mix_kernel(x_ref, preg_ref, win_ref, tab_ref, perm_ref, permt_ref, bbd_ref, cbd_ref, dskip_ref, wglu_ref,
                bglu_ref, lb_ref, hgng_ref, wg2_ref, bg_ref, glang_ref, kbd_ref, vbd_ref, wbr_ref, wout_ref,
                postg_ref, o_ref, sc_ref, carry_ref, hgst_ref, glast_ref):
    nb, tl, _ = x_ref.shape
    per_batch = [slice(bi * tl, (bi + 1) * tl) for bi in range(nb)]

    @pl.when(pl.program_id(1) == 0)
    def _():
        carry_ref[...] = jnp.zeros(carry_ref.shape, F32)
        hgst_ref[...] = jnp.zeros(hgst_ref.shape, F32)
        glast_ref[...] = jnp.zeros(glast_ref.shape, F32)

    hs = [_bf(_rmsnorm(x_ref[bi], preg_ref[...])) for bi in range(nb)]

    gate_logits = {}
    pending = [(kb, n, bi) for kb in range(N_BRANCH) for n in range(D_MODEL // GATE_PIECE) for bi in range(nb)]

    def fill(count):
        for _ in range(min(count, len(pending))):
            kb, n, bi = pending.pop(0)
            lo = C_GATE + kb * D_MODEL + n * GATE_PIECE
            gate_logits[(kb, n, bi)] = _dot(hs[bi], win_ref[:, lo:lo + GATE_PIECE])

    pin = jnp.concatenate([_dot(hb, win_ref[:, 0:C_GATE]) for hb in hs], axis=0)
    cols = lambda a, n: pin[:, a:a + n]
    u = cols(C_S5, BRANCH_W)
    ub = _bf(u)
    bus = [_dot(_bf(_dot(perm_ref[...], ub[r])), bbd_ref[...]) for r in per_batch]
    y_s5 = []
    for bi, r in enumerate(per_batch):
        fill(2)
        s_cat = _s5_scan(bus[bi], tab_ref, sc_ref.at[bi], carry_ref.at[bi])
        ysp = _dot(s_cat[:, :S5_W], cbd_ref[0:S5_W, :]) + _dot(s_cat[:, S5_W:], cbd_ref[S5_W:, :])
        ys3 = _dot(permt_ref[...], jnp.concatenate(_split3(ysp), axis=1))
        y = ys3[:, :BRANCH_W] + ys3[:, BRANCH_W:2 * BRANCH_W] + ys3[:, 2 * BRANCH_W:] + dskip_ref[...] * u[r]
        y = jax.nn.gelu(y)
        y_s5.append(y * jax.nn.sigmoid(_dot(_bf(y), wglu_ref[...]) + bglu_ref[...]))
    y_s5 = jnp.concatenate(y_s5, axis=0)

    hq = cols(C_HGQ, BRANCH_W)
    hf = cols(C_HGF, BRANCH_W)
    hi = cols(C_HGI, BRANCH_W)
    ho = cols(C_HGO, BRANCH_W)
    lb = lb_ref[...]
    hk = (1.0 - lb) * jax.nn.sigmoid(-hf)
    hlf = jnp.log(jnp.maximum(lb + (1.0 - lb) * jax.nn.sigmoid(hf), GATE_FLOOR))
    gq = cols(C_GQ, HEADS * GLA_DK) * (GLA_DK ** -0.5)
    gk = cols(C_GK, HEADS * GLA_DK)
    gv = cols(C_GV, BRANCH_W)
    gr = cols(C_GR, LANES)
    go = cols(C_GO, BRANCH_W)
    glf = _log_sigmoid(_dot(_bf(gr), wg2_ref[...]) + bg_ref[...]) * (1.0 / GLA_GATE_NORM)
    streams = [(hq[r], hk[r], hi[r], hlf[r], hgst_ref, bi, HG_DK) for bi, r in enumerate(per_batch)] + \
              [(gq[r], gk[r], gv[r], glf[r], glast_ref, bi, GLA_DK) for bi, r in enumerate(per_batch)]
    outs = _lin_attn(streams, functools.partial(fill, 3))
    y_hg = jnp.concatenate([_head_rmsnorm(o, hgng_ref[...]) for o in outs[:nb]], axis=0) * jax.nn.sigmoid(ho)
    y_gla = jnp.concatenate([_head_rmsnorm(o, glang_ref[...]) for o in outs[nb:]], axis=0) \
        * (go * jax.nn.sigmoid(go))

    xq = _bf(cols(C_XQ, BRANCH_W) * (XA_HD ** -0.5))
    lane_head = _iota((tl, BRANCH_W), 1) // XA_HD
    y_xa = []
    for bi, r in enumerate(per_batch):
        sc = _dot_nt(xq[r], kbd_ref[bi])
        m = sc.shape[1] // HEADS
        ps, invs = [], []
        for hh in range(HEADS):
            sh = sc[:, hh * m:(hh + 1) * m]
            p = jnp.exp(sh - jnp.max(sh, axis=-1, keepdims=True))
            ps.append(_bf(p))
            invs.append(1.0 / jnp.sum(p, axis=-1, keepdims=True))
        o_xa = _dot(jnp.concatenate(ps, axis=1), vbd_ref[bi])
        inv = jnp.where(lane_head == 0, invs[0],
                        jnp.where(lane_head == 1, invs[1], jnp.where(lane_head == 2, invs[2], invs[3])))
        y_xa.append(o_xa * inv)
    y_xa = jnp.concatenate(y_xa, axis=0)

    fill(len(pending))
    for bi, r in enumerate(per_batch):
        acc = None
        for kb, yb in enumerate((y_s5, y_hg, y_gla, y_xa)):
            gate = jax.nn.sigmoid(jnp.concatenate([gate_logits[(kb, n, bi)]
                                                   for n in range(D_MODEL // GATE_PIECE)], axis=1))
            term = gate * _dot(_bf(yb[r]), wbr_ref[kb])
            acc = term if acc is None else acc + term
        mixed = _dot(_bf(acc), wout_ref[...])
        o_ref[bi] = x_ref[bi] + _rmsnorm(mixed, postg_ref[...])


N_MIX_HEAD = 15


def _mix(x, layer, consts, perm, permt, kbd, vbd):
    bsz, seq, _ = x.shape
    tl, nb = ROW_TILE, BATCH_TILE
    x_spec = pl.BlockSpec((nb, tl, D_MODEL), lambda b, l: (b, l, 0))
    kv_spec = pl.BlockSpec((None, nb) + kbd.shape[2:], lambda b, l: (layer, b, 0, 0))
    perm_spec = pl.BlockSpec(perm.shape, lambda b, l: (0, 0), pipeline_mode=pl.Buffered(1))
    specs = [_layer_spec(c, layer) for c in consts]
    operands = list(consts[:3]) + [perm, permt] + list(consts[3:])
    specs = specs[:3] + [perm_spec, perm_spec] + specs[3:]
    operands = operands[:N_MIX_HEAD] + [kbd, vbd] + operands[N_MIX_HEAD:]
    specs = specs[:N_MIX_HEAD] + [kv_spec, kv_spec] + specs[N_MIX_HEAD:]
    scratch = [pltpu.VMEM((nb, 2, tl, S5_W), F32),
               pltpu.VMEM((nb, 2, SUBLANES, S5_W), F32),
               pltpu.VMEM((nb, BRANCH_W, HEADS * HG_DK), F32),
               pltpu.VMEM((nb, BRANCH_W, HEADS * GLA_DK), F32)]
    resident = sum(_nbytes(c.shape[1:], c.dtype) for c in consts) + 2 * _nbytes(perm.shape, perm.dtype) \
        + nb * (4 * _nbytes(kbd.shape[2:], kbd.dtype) + 4 * _nbytes((tl, D_MODEL), F32) + _nbytes((2, tl, S5_W), F32))
    temporaries = nb * 6 * _nbytes((tl, 2 * S5_W), F32)
    return pl.pallas_call(
        _mix_kernel,
        grid=(bsz // nb, seq // tl),
        in_specs=[x_spec] + specs,
        out_specs=x_spec,
        out_shape=jax.ShapeDtypeStruct(x.shape, F32),
        scratch_shapes=scratch,
        compiler_params=pltpu.CompilerParams(dimension_semantics=("arbitrary", "arbitrary"),
                                             vmem_limit_bytes=resident + temporaries),
        name="mix",
    )(x, *operands)


def _ffn_kernel(x_ref, preg_ref, wa_ref, wu_ref, cw_ref, cb_ref, wo_ref, postg_ref, o_ref, a_ref):
    nb, tl, _ = x_ref.shape

    @pl.when(pl.program_id(1) == 0)
    def _():
        a_ref[:, 0:SUBLANES, :] = jnp.zeros((nb, SUBLANES, D_FF), F32)

    hs = [_bf(_rmsnorm(x_ref[bi], preg_ref[...])) for bi in range(nb)]
    au = [(_dot(h, wa_ref[...]), _dot(h, wu_ref[...])) for h in hs]
    cw = cw_ref[...]
    acts = []
    for bi, (a, u) in enumerate(au):
        a_ref[bi, SUBLANES:SUBLANES + tl, :] = a
        conv = (cw[2:3] * a + cw[1:2] * a_ref[bi, SUBLANES - 1:SUBLANES - 1 + tl, :]
                + cw[0:1] * a_ref[bi, SUBLANES - 2:SUBLANES - 2 + tl, :] + cb_ref[...])
        a_ref[bi, 0:SUBLANES, :] = a[tl - SUBLANES:tl]
        acts.append(_bf(conv * jax.nn.sigmoid(conv) * u))
    fs = [_dot(act, wo_ref[...]) for act in acts]
    for bi, f in enumerate(fs):
        o_ref[bi] = x_ref[bi] + _rmsnorm(f, postg_ref[...])


def _ffn(x, layer, pre_g, w_in, conv_w, conv_b, w_o, post_g):
    bsz, seq, _ = x.shape
    tl, nb = ROW_TILE, BATCH_TILE
    x_spec = pl.BlockSpec((nb, tl, D_MODEL), lambda b, l: (b, l, 0))
    half_spec = lambda j: pl.BlockSpec((None, D_MODEL, D_FF), lambda b, l: (layer, 0, j),
                                       pipeline_mode=pl.Buffered(1))
    operands = (pre_g, w_in, w_in, conv_w, conv_b, w_o, post_g)
    specs = [_layer_spec(pre_g, layer), half_spec(0), half_spec(1), _layer_spec(conv_w, layer),
             _layer_spec(conv_b, layer), _layer_spec(w_o, layer), _layer_spec(post_g, layer)]
    resident = sum(_nbytes(c.shape[1:], c.dtype) for c in operands[2:]) \
        + nb * (4 * _nbytes((tl, D_MODEL), F32) + _nbytes((SUBLANES + tl, D_FF), F32))
    temporaries = nb * 4 * _nbytes((tl, D_FF), F32)
    return pl.pallas_call(
        _ffn_kernel,
        grid=(bsz // nb, seq // tl),
        in_specs=[x_spec] + specs,
        out_specs=x_spec,
        out_shape=jax.ShapeDtypeStruct(x.shape, F32),
        scratch_shapes=[pltpu.VMEM((nb, SUBLANES + tl, D_FF), F32)],
        compiler_params=pltpu.CompilerParams(dimension_semantics=("arbitrary", "arbitrary"),
                                             vmem_limit_bytes=resident + temporaries),
        name="ffn",
    )(x, *operands)


def _block_diag_rows(t):
    same = (jnp.arange(S5_GROUPS)[:, None, None] == (jnp.arange(S5_W) // S5_STATE)[None, None, :])
    return jnp.where(same[None], t[:, None], 0.0).reshape(t.shape[0], BRANCH_W, S5_W)


def kernel(x, mem, pre_mix_g, w_in, s5_lam_re, s5_lam_im, s5_log_dt, s5_b_re, s5_b_im, s5_c_re, s5_c_im, s5_d, s5_w_glu, s5_b_glu, hg_lb_logits, hg_norm_g, gla_w_g2, gla_b_g, gla_norm_g, mem_norm_g, w_mem_kv, w_branch, w_out, post_mix_g, pre_ffn_g, w_ffn_in, ffn_conv_w, ffn_conv_b, w_ffn_out, post_ffn_g):
    out_dtype = x.dtype
    bsz, seq, _ = x.shape
    assert seq % ROW_TILE == 0 and bsz % BATCH_TILE == 0 and x.shape[2] == D_MODEL
    depth = w_in.shape[0]
    x = x.astype(F32)
    rows = lambda t: t.astype(F32).reshape(depth, 1, -1)
    tab, bb, lower = _prep(s5_lam_re, s5_lam_im, s5_log_dt, s5_b_re, s5_b_im, hg_lb_logits)
    kbd, vbd = _memkv(mem.astype(F32), rows(mem_norm_g), _bf(w_mem_kv))
    tok = jnp.arange(ROW_TILE)
    perm = _bf(((tok % SEG) * SUBLANES + tok // SEG)[None, :] == tok[:, None])
    w_in_b = _bf(w_in)
    w_in_p = jnp.concatenate([w_in_b[:, :, :RAW_AFTER_GR], jnp.zeros((depth, D_MODEL, LANES - GLA_RANK), BF16),
                              w_in_b[:, :, RAW_AFTER_GR:]], axis=2)
    bbd = _bf(jnp.concatenate([_block_diag_rows(bb[:, 0]), _block_diag_rows(bb[:, 1])], axis=2))
    ct = lambda t: jnp.transpose(t.astype(F32), (0, 2, 1, 3)).reshape(depth, S5_GROUP, S5_W)
    cbd = _bf(jnp.concatenate([jnp.swapaxes(_block_diag_rows(ct(s5_c_re)), 1, 2),
                               -jnp.swapaxes(_block_diag_rows(ct(s5_c_im)), 1, 2)], axis=1))
    wg2 = _bf(jnp.concatenate([gla_w_g2, jnp.zeros((depth, LANES - GLA_RANK, HEADS * GLA_DK), gla_w_g2.dtype)],
                              axis=1))
    mix_consts = (rows(pre_mix_g), w_in_p, tab, bbd, cbd, rows(s5_d), _bf(s5_w_glu), rows(s5_b_glu), lower,
                  rows(hg_norm_g), wg2, rows(gla_b_g), rows(gla_norm_g), _bf(w_branch), _bf(w_out),
                  rows(post_mix_g))
    ffn_consts = (rows(pre_ffn_g), _bf(w_ffn_in), ffn_conv_w.astype(F32), rows(ffn_conv_b), _bf(w_ffn_out),
                  rows(post_ffn_g))
    for layer in range(depth):
        x = _mix(x, layer, mix_consts, perm, perm.T, kbd, vbd)
        x = _ffn(x, layer, *ffn_consts)
    return x.astype(out_dtype)
```
